```python
import jax, jax.numpy as jnp
from jax import lax
import numpy as np

D_MODEL = 2048
BATCH = 2
SEQ = 4096
DEPTH = 2
DEC_BATCH = 32
DEC_SEQ = 8
PAST_LEN = 16384
PAGE_SIZE = 128

N_EVEN = (DEPTH + 1) // 2
N_ODD = DEPTH // 2
EPS = 1e-6
NEG = -1e30

H_A = 16
N_A = 64
W_A = H_A * N_A
LORA_W = 64
LORA_A = 64
SHIFT_W = 3 * W_A + LORA_W + LORA_A
GN_EPS = 64e-5
H_B = 16
KV_B = 2
HD_B = 64
W_B = H_B * HD_B
WINDOW = 128
BLK = 128
H_C = 16
HD_C = 128
W_C = H_C * HD_C
Q_BLK = 128
N_MEM = 256
H_M = 4
HD_M = 64
W_M = H_M * HD_M

IN_E = SHIFT_W + W_A + 2 * W_B + 2 * KV_B * HD_B + 2 * W_M
OUT_E = W_A + W_B + W_M
IN_O = 4 * W_C + H_C + 2 * W_M
OUT_O = W_C + W_M

kernel_name = 'hybrid_rwkv7_swa_fox_memory_decode_step'

f32 = jnp.float32


def split_cols(z, sizes):
    idx = [int(i) for i in np.cumsum(sizes)[:-1]]
    return jnp.split(z, idx, axis=-1)


def rmsnorm(x, g):
    xf = x.astype(f32)
    y = xf * lax.rsqrt(jnp.mean(xf * xf, axis=-1, keepdims=True) + EPS)
    return (y * g.astype(f32)).astype(x.dtype)


def alibi_slopes(n):
    return 2.0 ** (-8.0 * jnp.arange(1, n + 1, dtype=f32) / n)


def softmax_with_sink(logits, sink):
    m = jnp.maximum(jnp.max(logits, axis=-1, keepdims=True), sink)
    p = jnp.exp(logits - m)
    return p / (jnp.sum(p, axis=-1, keepdims=True) + jnp.exp(sink - m))


def rwkv7_mix(cols, gate, shift_prev, s0, mu, w0, w_up, a0, a_up, k_k, k_a, r_k, ln_g, ln_b):
    B, T, _ = cols.shape
    c = cols.astype(f32)
    prev = jnp.concatenate([shift_prev.astype(f32)[:, None], c[:, :-1]], axis=1)
    xs = c + (prev - c) * mu.astype(f32)
    r, k, v, wlo, alo = split_cols(xs, [W_A, W_A, W_A, LORA_W, LORA_A])
    w_raw = w0.astype(f32) + jnp.tanh(wlo) @ w_up.astype(f32)
    decay = jnp.exp(-jnp.exp(-jax.nn.softplus(-w_raw) - 0.5))
    a = jax.nn.sigmoid(a0.astype(f32) + alo @ a_up.astype(f32))
    kk = k * k_k.astype(f32)
    k = k * (1.0 + (a - 1.0) * k_a.astype(f32))
    heads = lambda t: t.reshape(B, T, H_A, N_A)
    r, k, v, decay, a, kk = (heads(t) for t in (r, k, v, decay, a, kk))
    kk = kk / jnp.maximum(jnp.sqrt(jnp.sum(kk * kk, axis=-1, keepdims=True)), 1e-12)

    def step(S, inp):
        r_t, k_t, v_t, w_t, kk_t, a_t = inp
        sa = jnp.einsum('bhij,bhj->bhi', S, -kk_t)
        S = (S * w_t[:, :, None, :] + sa[..., None] * (kk_t * a_t)[:, :, None, :]
             + v_t[..., None] * k_t[:, :, None, :])
        return S, jnp.einsum('bhij,bhj->bhi', S, r_t)

    seq_first = lambda t: jnp.moveaxis(t, 1, 0)
    S, y = lax.scan(step, s0.astype(f32), tuple(seq_first(t) for t in (r, k, v, decay, kk, a)))
    y = jnp.moveaxis(y, 0, 1)
    mean = jnp.mean(y, axis=-1, keepdims=True)
    var = jnp.mean(jnp.square(y - mean), axis=-1, keepdims=True)
    yn = ((y - mean) * lax.rsqrt(var + GN_EPS)).reshape(B, T, W_A) * ln_g.astype(f32) + ln_b.astype(f32)
    bonus = (jnp.sum(r * k * r_k.astype(f32), axis=-1, keepdims=True) * v).reshape(B, T, W_A)
    out = (yn + bonus) * jax.nn.silu(gate.astype(f32))
    return out, cols[:, -1], S.astype(s0.dtype)


def swa_prompt(q, k, v, sink):
    B, T = q.shape[:2]
    NB = T // BLK
    G = H_B // KV_B
    qb = q.astype(f32).reshape(B, NB, BLK, KV_B, G, HD_B)
    kb = k.astype(f32).reshape(B, NB, BLK, KV_B, HD_B)
    vb = v.astype(f32).reshape(B, NB, BLK, KV_B, HD_B)
    pad = ((0, 0), (1, 0), (0, 0), (0, 0), (0, 0))
    kband = jnp.concatenate([jnp.pad(kb, pad)[:, :-1], kb], axis=2)
    vband = jnp.concatenate([jnp.pad(vb, pad)[:, :-1], vb], axis=2)
    logits = jnp.einsum('bnqkgd,bnskd->bnkgqs', qb, kband) * (HD_B ** -0.5)
    i = jnp.arange(BLK)[:, None]
    j = jnp.arange(2 * BLK)[None, :]
    dist = i + BLK - j
    band = (dist >= 0) & (dist < WINDOW)
    valid = band[None] & ((jnp.arange(NB)[:, None, None] > 0) | (j >= BLK)[None])
    slopes = alibi_slopes(H_B).reshape(KV_B, G)[:, :, None, None]
    logits = logits - slopes * dist.astype(f32)
    logits = jnp.where(valid[None, :, None, None], logits, NEG)
    p = softmax_with_sink(logits, sink.astype(f32).reshape(KV_B, G)[:, :, None, None])
    o = jnp.einsum('bnkgqs,bnskd->bnqkgd', p, vband)
    return o.reshape(B, T, W_B), k[:, -WINDOW:], v[:, -WINDOW:]


def swa_decode(q, k, v, kbuf, vbuf, sink):
    B, S = q.shape[:2]
    G = H_B // KV_B
    kall = jnp.concatenate([kbuf.astype(k.dtype), k], axis=1)
    vall = jnp.concatenate([vbuf.astype(v.dtype), v], axis=1)
    qg = q.astype(f32).reshape(B, S, KV_B, G, HD_B)
    logits = jnp.einsum('bqkgd,bskd->bkgqs', qg, kall.astype(f32)) * (HD_B ** -0.5)
    dist = jnp.arange(S)[:, None] + WINDOW - jnp.arange(WINDOW + S)[None, :]
    valid = (dist >= 0) & (dist < WINDOW)
    slopes = alibi_slopes(H_B).reshape(KV_B, G)[:, :, None, None]
    logits = jnp.where(valid, logits - slopes * dist.astype(f32), NEG)
    p = softmax_with_sink(logits, sink.astype(f32).reshape(KV_B, G)[:, :, None, None])
    o = jnp.einsum('bkgqs,bskd->bqkgd', p, vall.astype(f32))
    return o.reshape(B, S, W_B), kall[:, -WINDOW:], vall[:, -WINDOW:]


def mem_kv(mem, g, wk, wv):
    hm = rmsnorm(mem, g)
    B = mem.shape[0]
    return (hm @ wk).reshape(B, N_MEM, H_M, HD_M), (hm @ wv).reshape(B, N_MEM, H_M, HD_M)


def mem_attend(q, mk, mv):
    logits = jnp.einsum('bthd,bshd->bhts', q.astype(f32), mk.astype(f32)) * (HD_M ** -0.5)
    p = jax.nn.softmax(logits, axis=-1)
    B, T = q.shape[:2]
    return jnp.einsum('bhts,bshd->bthd', p, mv.astype(f32)).reshape(B, T, W_M)


def fox_prompt(q, k, v, logf):
    B, T = q.shape[:2]
    NB = T // Q_BLK
    kf = k.astype(f32)
    vf = v.astype(f32)
    c = jnp.cumsum(logf, axis=1)
    c_keys = jnp.moveaxis(c, 1, 2)
    qb = jnp.moveaxis((q.astype(f32) * (HD_C ** -0.5)).reshape(B, NB, Q_BLK, H_C, HD_C), 1, 0)
    cb = jnp.moveaxis(c.reshape(B, NB, Q_BLK, H_C), 1, 0)
    s_pos = jnp.arange(T)

    def block(args):
        q_blk, c_blk, n = args
        logits = jnp.einsum('bqhd,bshd->bhqs', q_blk, kf)
        logits = logits + (jnp.moveaxis(c_blk, 1, 2)[..., None] - c_keys[:, :, None, :])
        t_pos = n * Q_BLK + jnp.arange(Q_BLK)
        logits = jnp.where(s_pos[None, :] <= t_pos[:, None], logits, NEG)
        p = jax.nn.softmax(logits, axis=-1)
        return jnp.einsum('bhqs,bshd->bqhd', p, vf)

    o = lax.map(block, (qb, cb, jnp.arange(NB)))
    return jnp.moveaxis(o, 0, 1).reshape(B, T, W_C)


def fox_decode(q, k, v, logf, cache_k, cache_v, cache_logf, page_table, layer_idx):
    B, S = q.shape[:2]
    n_pages = page_table.shape[1]
    P = n_pages * PAGE_SIZE
    lf_past = cache_logf[layer_idx, page_table].reshape(B, P, H_C).astype(f32)
    C = jnp.cumsum(jnp.concatenate([lf_past, logf], axis=1), axis=1)
    cq = jnp.moveaxis(C[:, P:], 1, 2)
    qf = q.astype(f32) * (HD_C ** -0.5)
    c_pages = jnp.moveaxis(C[:, :P].reshape(B, n_pages, PAGE_SIZE, H_C), 1, 0)

    def accumulate(carry, logits, vals):
        m, l, acc = carry
        m_new = jnp.maximum(m, jnp.max(logits, axis=-1))
        alpha = jnp.exp(m - m_new)
        p = jnp.exp(logits - m_new[..., None])
        l = l * alpha + jnp.sum(p, axis=-1)
        acc = acc * alpha[..., None] + jnp.einsum('bhqs,bshd->bhqd', p, vals)
        return (m_new, l, acc)

    def page_step(carry, inp):
        pt, c_pg = inp
        kp = cache_k[layer_idx, pt].astype(f32)
        vp = cache_v[layer_idx, pt].astype(f32)
        logits = jnp.einsum('bqhd,bshd->bhqs', qf, kp)
        logits = logits + (cq[..., None] - jnp.moveaxis(c_pg, 1, 2)[:, :, None, :])
        return accumulate(carry, logits, vp), None

    init = (jnp.full((B, H_C, S), NEG, f32), jnp.zeros((B, H_C, S), f32),
            jnp.zeros((B, H_C, S, HD_C), f32))
    carry, _ = lax.scan(page_step, init, (page_table.T, c_pages))
    logits = jnp.einsum('bqhd,bshd->bhqs', qf, k.astype(f32))
    logits = logits + (cq[..., None] - cq[:, :, None, :])
    logits = jnp.where(jnp.tril(jnp.ones((S, S), dtype=bool)), logits, NEG)
    m, l, acc = accumulate(carry, logits, v.astype(f32))
    o = acc / l[..., None]
    return jnp.moveaxis(o, 1, 2).reshape(B, S, W_C)


def even_layer(x, mk, mv, shift_prev, s0, swa_buf, norm_g, w_in, rw, sink, w_out):
    B, T = x.shape[:2]
    z = rmsnorm(x, norm_g) @ w_in
    cols, gate_a, q_b, k_b, v_b, gate_b, q_m, gate_m = split_cols(
        z, [SHIFT_W, W_A, W_B, KV_B * HD_B, KV_B * HD_B, W_B, W_M, W_M])
    a_out, new_shift, new_s = rwkv7_mix(cols, gate_a, shift_prev, s0, *rw)
    q_b = q_b.reshape(B, T, H_B, HD_B)
    k_b = k_b.reshape(B, T, KV_B, HD_B)
    v_b = v_b.reshape(B, T, KV_B, HD_B)
    if swa_buf is None:
        b_out, kw, vw = swa_prompt(q_b, k_b, v_b, sink)
    else:
        b_out, kw, vw = swa_decode(q_b, k_b, v_b, swa_buf[0], swa_buf[1], sink)
    b_out = b_out * jax.nn.silu(gate_b.astype(f32))
    m_out = mem_attend(q_m.reshape(B, T, H_M, HD_M), mk, mv) * jax.nn.silu(gate_m.astype(f32))
    y = jnp.concatenate([a_out, b_out, m_out], axis=-1).astype(x.dtype) @ w_out
    return x + y, (new_shift, new_s, kw, vw)


def odd_layer(x, mk, mv, fox_past, norm_g, w_in, b_f, w_out):
    B, T = x.shape[:2]
    z = rmsnorm(x, norm_g) @ w_in
    q, k, v, f, gate_c, q_m, gate_m = split_cols(z, [W_C, W_C, W_C, H_C, W_C, W_M, W_M])
    q = q.reshape(B, T, H_C, HD_C)
    k = k.reshape(B, T, H_C, HD_C)
    v = v.reshape(B, T, H_C, HD_C)
    logf = jax.nn.log_sigmoid(f.astype(f32) + b_f.astype(f32))
    if fox_past is None:
        c_out = fox_prompt(q, k, v, logf)
    else:
        c_out = fox_decode(q, k, v, logf, *fox_past)
    c_out = c_out * jax.nn.silu(gate_c.astype(f32))
    m_out = mem_attend(q_m.reshape(B, T, H_M, HD_M), mk, mv) * jax.nn.silu(gate_m.astype(f32))
    y = jnp.concatenate([c_out, m_out], axis=-1).astype(x.dtype) @ w_out
    return x + y, (k, v, logf)


def setup_inputs(seed: int = 0) -> dict:
    key = jax.random.key(seed)
    ks = iter(jax.random.split(key, 48))
    normal = lambda shape, scale=1.0: scale * jax.random.normal(next(ks), shape, f32)
    uniform = lambda shape, lo, hi: jax.random.uniform(next(ks), shape, f32, minval=lo, maxval=hi)
    n_pages = PAST_LEN // PAGE_SIZE
    n_used = DEC_BATCH * n_pages
    n_pool = n_used + n_used // 4
    E, O = N_EVEN, N_ODD
    page_table = jax.random.permutation(next(ks), n_pool)[:n_used].reshape(DEC_BATCH, n_pages).astype(jnp.int32)
    return {
        'x_prompt': normal((BATCH, SEQ, D_MODEL)),
        'x_sample': normal((DEC_BATCH, DEC_SEQ, D_MODEL)),
        'state_rwkv': normal((E, DEC_BATCH, H_A, N_A, N_A)),
        'state_shift': normal((E, DEC_BATCH, SHIFT_W)),
        'cache_swa_k': normal((E, DEC_BATCH, WINDOW, KV_B, HD_B)),
        'cache_swa_v': normal((E, DEC_BATCH, WINDOW, KV_B, HD_B)),
        'cache_mem_k': normal((DEPTH, DEC_BATCH, N_MEM, H_M, HD_M)),
        'cache_mem_v': normal((DEPTH, DEC_BATCH, N_MEM, H_M, HD_M)),
        'cache_fox_k': normal((O, n_pool, PAGE_SIZE, H_C, HD_C)),
        'cache_fox_v': normal((O, n_pool, PAGE_SIZE, H_C, HD_C)),
        'cache_fox_logf': jax.nn.log_sigmoid(normal((O, n_pool, PAGE_SIZE, H_C)) + 2.5),
        'page_table': page_table,
        'mem_prompt': normal((BATCH, N_MEM, D_MODEL)),
        'norm_e': 1.0 + normal((E, D_MODEL), 0.01),
        'w_in_e': normal((E, D_MODEL, IN_E), D_MODEL ** -0.5),
        'rwkv_mu': uniform((E, SHIFT_W), 0.0, 1.0),
        'rwkv_w0': uniform((E, W_A), -6.0, -1.0),
        'rwkv_w_up': normal((E, LORA_W, W_A), 0.5 * LORA_W ** -0.5),
        'rwkv_a0': normal((E, W_A), 0.1),
        'rwkv_a_up': normal((E, LORA_A, W_A), 0.5 * LORA_A ** -0.5),
        'rwkv_kk': 1.0 + normal((E, W_A), 0.1),
        'rwkv_ka': 1.0 + normal((E, W_A), 0.1),
        'rwkv_rk': normal((E, H_A, N_A), 0.1),
        'rwkv_ln_g': 1.0 + normal((E, W_A), 0.01),
        'rwkv_ln_b': normal((E, W_A), 0.01),
        'swa_sink': normal((E, H_B), 0.5),
        'w_out_e': normal((E, OUT_E, D_MODEL), 0.5 * OUT_E ** -0.5),
        'norm_o': 1.0 + normal((O, D_MODEL), 0.01),
        'w_in_o': normal((O, D_MODEL, IN_O), D_MODEL ** -0.5),
        'fox_bf': uniform((O, H_C), 1.0, 4.0),
        'w_out_o': normal((O, OUT_O, D_MODEL), 0.5 * OUT_O ** -0.5),
        'mem_norm': 1.0 + normal((DEPTH, D_MODEL), 0.01),
        'w_mem_k': normal((DEPTH, D_MODEL, W_M), D_MODEL ** -0.5),
        'w_mem_v': normal((DEPTH, D_MODEL, W_M), D_MODEL ** -0.5),
        'norm_f': 1.0 + normal((D_MODEL,), 0.01),
    }


def reference(x_prompt, x_sample, state_rwkv, state_shift, cache_swa_k, cache_swa_v, cache_mem_k,
              cache_mem_v, cache_fox_k, cache_fox_v, cache_fox_logf, page_table, mem_prompt,
              norm_e, w_in_e, rwkv_mu, rwkv_w0, rwkv_w_up, rwkv_a0, rwkv_a_up, rwkv_kk, rwkv_ka,
              rwkv_rk, rwkv_ln_g, rwkv_ln_b, swa_sink, w_out_e, norm_o, w_in_o, fox_bf, w_out_o,
              mem_norm, w_mem_k, w_mem_v, norm_f):
    def rwkv_params(e):
        return (rwkv_mu[e], rwkv_w0[e], rwkv_w_up[e], rwkv_a0[e], rwkv_a_up[e], rwkv_kk[e],
                rwkv_ka[e], rwkv_rk[e], rwkv_ln_g[e], rwkv_ln_b[e])

    xp = x_prompt
    p_rwkv, p_shift, p_swa_k, p_swa_v, p_mem_k, p_mem_v = [], [], [], [], [], []
    p_fox_k, p_fox_v, p_fox_lf = [], [], []
    for layer in range(DEPTH):
        mk, mv = mem_kv(mem_prompt, mem_norm[layer], w_mem_k[layer], w_mem_v[layer])
        p_mem_k.append(mk)
        p_mem_v.append(mv)
        if layer % 2 == 0:
            e = layer // 2
            shift0 = jnp.zeros((BATCH, SHIFT_W), xp.dtype)
            s0 = jnp.zeros((BATCH, H_A, N_A, N_A), xp.dtype)
            xp, (sh, st, kw, vw) = even_layer(xp, mk, mv, shift0, s0, None, norm_e[e], w_in_e[e],
                                              rwkv_params(e), swa_sink[e], w_out_e[e])
            p_shift.append(sh)
            p_rwkv.append(st)
            p_swa_k.append(kw)
            p_swa_v.append(vw)
        else:
            o = layer // 2
            xp, (kc, vc, lf) = odd_layer(xp, mk, mv, None, norm_o[o], w_in_o[o], fox_bf[o], w_out_o[o])
            p_fox_k.append(kc)
            p_fox_v.append(vc)
            p_fox_lf.append(lf)
    y_prompt = rmsnorm(xp, norm_f)

    xs = x_sample
    s_rwkv, s_shift, s_swa_k, s_swa_v, s_fox_k, s_fox_v, s_fox_lf = [], [], [], [], [], [], []
    for layer in range(DEPTH):
        mk, mv = cache_mem_k[layer], cache_mem_v[layer]
        if layer % 2 == 0:
            e = layer // 2
            xs, (sh, st, kw, vw) = even_layer(xs, mk, mv, state_shift[e], state_rwkv[e],
                                              (cache_swa_k[e], cache_swa_v[e]), norm_e[e], w_in_e[e],
                                              rwkv_params(e), swa_sink[e], w_out_e[e])
            s_shift.append(sh)
            s_rwkv.append(st)
            s_swa_k.append(kw)
            s_swa_v.append(vw)
        else:
            o = layer // 2
            xs, (kc, vc, lf) = odd_layer(xs, mk, mv,
                                         (cache_fox_k, cache_fox_v, cache_fox_logf, page_table, o),
                                         norm_o[o], w_in_o[o], fox_bf[o], w_out_o[o])
            s_fox_k.append(kc)
            s_fox_v.append(vc)
            s_fox_lf.append(lf)
    y_sample = rmsnorm(xs, norm_f)

    return (y_prompt, y_sample,
            jnp.stack(p_rwkv), jnp.stack(p_shift), jnp.stack(p_swa_k), jnp.stack(p_swa_v),
            jnp.stack(p_mem_k), jnp.stack(p_mem_v),
            jnp.stack(p_fox_k), jnp.stack(p_fox_v), jnp.stack(p_fox_lf),
            jnp.stack(s_rwkv), jnp.stack(s_shift), jnp.stack(s_swa_k), jnp.stack(s_swa_v),
            jnp.stack(s_fox_k), jnp.stack(s_fox_v), jnp.stack(s_fox_lf))
```

```python
import functools

import jax
import jax.numpy as jnp
import numpy as np
from jax import lax
from jax.experimental import pallas as pl
from jax.experimental.pallas import tpu as pltpu

F32 = jnp.float32
BF16 = jnp.bfloat16
HIGHEST = lax.Precision.HIGHEST

D_MODEL = 2048
EPS = 1e-6
NEG = -1e30
H_A, N_A = 16, 64
W_A = H_A * N_A
LORA = 64
SHIFT_W = 3 * W_A + 2 * LORA
GN_EPS = 64e-5
CHUNK = 64
H_B, KV_B, HD_B = 16, 2, 64
W_B = H_B * HD_B
WINDOW = 128
H_C, HD_C = 16, 128
W_C = H_C * HD_C
PAGE = 128
N_MEM, H_M, HD_M = 256, 4, 64
W_M = H_M * HD_M

LANES = 128
VMEM_LIMIT = 48 * 1024 * 1024


def _cparams(sem):
    return pltpu.CompilerParams(dimension_semantics=sem, vmem_limit_bytes=VMEM_LIMIT)


def _dot(a, b):
    return jnp.dot(a, b, preferred_element_type=F32)


def _dot_nt(a, b):
    return lax.dot_general(a, b, (((1,), (1,)), ((), ())), preferred_element_type=F32)


def _dot_tn(a, b):
    return lax.dot_general(a, b, (((0,), (0,)), ((), ())), preferred_element_type=F32)


def _dot_hi(a, b):
    return jnp.dot(a, b, preferred_element_type=F32, precision=HIGHEST)


def _silu(g):
    return g * jax.nn.sigmoid(g)


def _largest_tile(n, cap, mult):
    best = None
    for t in range(mult, min(n, cap) + 1, mult):
        if n % t == 0:
            best = t
    assert best is not None, (n, cap, mult)
    return best


def _rmsnorm_kernel(x_ref, g_ref, o_ref):
    x = x_ref[...]
    ms = jnp.mean(x * x, axis=-1, keepdims=True)
    o_ref[...] = (x * lax.rsqrt(ms + EPS) * g_ref[...]).astype(o_ref.dtype)


def rmsnorm_cast(x, g):
    m, d = x.shape
    tm = _largest_tile(m, 512, 8)
    return pl.pallas_call(
        _rmsnorm_kernel,
        out_shape=jax.ShapeDtypeStruct((m, d), BF16),
        grid=(m // tm,),
        in_specs=[pl.BlockSpec((tm, d), lambda i: (i, 0)),
                  pl.BlockSpec((1, d), lambda i: (0, 0))],
        out_specs=pl.BlockSpec((tm, d), lambda i: (i, 0)),
        compiler_params=_cparams(("parallel",)),
        name="rmsnorm_cast",
    )(x, g.reshape(1, d))


def _matmul_kernel(x_ref, w_ref, o_ref):
    o_ref[...] = _dot(x_ref[...], w_ref[...])


def matmul(x, w):
    m, k = x.shape
    n = w.shape[1]
    tm = _largest_tile(m, 1024, 8)
    tn = _largest_tile(n, 1536, LANES)
    return pl.pallas_call(
        _matmul_kernel,
        out_shape=jax.ShapeDtypeStruct((m, n), F32),
        grid=(m // tm, n // tn),
        in_specs=[pl.BlockSpec((tm, k), lambda i, j: (i, 0)),
                  pl.BlockSpec((k, tn), lambda i, j: (0, j))],
        out_specs=pl.BlockSpec((tm, tn), lambda i, j: (i, j)),
        compiler_params=_cparams(("parallel", "parallel")),
        name="matmul",
    )(x, w)


def _outproj_kernel(*refs, n_in, final):
    hs = refs[:n_in]
    ws = refs[n_in:2 * n_in]
    x_ref, g_ref = refs[2 * n_in], refs[2 * n_in + 1]
    outs = refs[2 * n_in + 2:]
    acc = x_ref[...]
    for h_ref, w_ref in zip(hs, ws):
        acc = acc + _dot(h_ref[...].astype(BF16), w_ref[...])
    ms = jnp.mean(acc * acc, axis=-1, keepdims=True)
    normed = acc * lax.rsqrt(ms + EPS) * g_ref[...]
    if final:
        outs[0][...] = normed
    else:
        outs[0][...] = acc
        outs[1][...] = normed.astype(BF16)


def outproj(hs, ws, x, g, final):
    m, d = x.shape
    tm = _largest_tile(m, 256, 16)
    n_in = len(hs)
    in_specs = [pl.BlockSpec((tm, h.shape[1]), lambda i: (i, 0)) for h in hs]
    in_specs += [pl.BlockSpec(w.shape, lambda i: (0, 0)) for w in ws]
    in_specs += [pl.BlockSpec((tm, d), lambda i: (i, 0)), pl.BlockSpec((1, d), lambda i: (0, 0))]
    row = pl.BlockSpec((tm, d), lambda i: (i, 0))
    if final:
        out_shape, out_specs = jax.ShapeDtypeStruct((m, d), F32), row
    else:
        out_shape = (jax.ShapeDtypeStruct((m, d), F32), jax.ShapeDtypeStruct((m, d), BF16))
        out_specs = (row, row)
    return pl.pallas_call(
        functools.partial(_outproj_kernel, n_in=n_in, final=final),
        out_shape=out_shape,
        grid=(m // tm,),
        in_specs=in_specs,
        out_specs=out_specs,
        compiler_params=_cparams(("parallel",)),
        name="outproj",
    )(*hs, *ws, x, g.reshape(1, d))


def _mem_attn_kernel(q_ref, g_ref, k_ref, v_ref, o_ref):
    tq = q_ref.shape[0]
    lane = lax.broadcasted_iota(jnp.int32, (tq, LANES), 1)
    first = lane < HD_M
    outs = []
    for p in range(W_M // LANES):
        sl = slice(p * LANES, (p + 1) * LANES)
        q = q_ref[:, sl] * (HD_M ** -0.5)
        kp = k_ref[:, sl].astype(BF16)
        vp = v_ref[:, sl].astype(BF16)
        o_pair = []
        for sel in (first, jnp.logical_not(first)):
            s = _dot_nt(jnp.where(sel, q, 0.0).astype(BF16), kp)
            m = jnp.max(s, axis=-1, keepdims=True)
            e = jnp.exp(s - m)
            pr = e / jnp.sum(e, axis=-1, keepdims=True)
            o_pair.append(_dot(pr.astype(BF16), vp))
        o = jnp.where(first, o_pair[0], o_pair[1])
        outs.append(o * _silu(g_ref[:, sl]))
    o_ref[...] = jnp.concatenate(outs, axis=-1).astype(o_ref.dtype)


def mem_attn(z, q_col, g_col, mk, mv, n_batch, t, out_dtype):
    tq = _largest_tile(t, 512, 8)
    nq = t // tq
    return pl.pallas_call(
        _mem_attn_kernel,
        out_shape=jax.ShapeDtypeStruct((n_batch * t, W_M), out_dtype),
        grid=(n_batch, nq),
        in_specs=[pl.BlockSpec((tq, W_M), lambda b, i: (b * nq + i, q_col)),
                  pl.BlockSpec((tq, W_M), lambda b, i: (b * nq + i, g_col)),
                  pl.BlockSpec((None, N_MEM, W_M), lambda b, i: (b, 0, 0)),
                  pl.BlockSpec((None, N_MEM, W_M), lambda b, i: (b, 0, 0))],
        out_specs=pl.BlockSpec((tq, W_M), lambda b, i: (b * nq + i, 0)),
        compiler_params=_cparams(("parallel", "parallel")),
        name="mem_attn",
    )(z, z, mk, mv)


def _alibi_slope(h):
    return float(2.0 ** (-8.0 * (h + 1) / H_B))


def _swa_kernel(sink_ref, q_ref, g_ref, kp_ref, kc_ref, vp_ref, vc_ref, o_ref, *, prompt):
    tq = q_ref.shape[0]
    nk = 2 * WINDOW
    lane = lax.broadcasted_iota(jnp.int32, (tq, LANES), 1)
    first = lane < HD_B
    klane = lax.broadcasted_iota(jnp.int32, (nk, LANES), 1)
    qi = lax.broadcasted_iota(jnp.int32, (tq, nk), 0)
    kj = lax.broadcasted_iota(jnp.int32, (tq, nk), 1)
    dist = qi + WINDOW - kj
    valid = (dist >= 0) & (dist < WINDOW)
    if prompt:
        has_prev = pl.program_id(1) > 0
        valid = valid & (has_prev | (kj >= WINDOW))
    distf = dist.astype(F32)
    kband = jnp.concatenate([kp_ref[...], kc_ref[...]], axis=0)
    vband = jnp.concatenate([vp_ref[...], vc_ref[...]], axis=0)
    kroll = pltpu.roll(kband, HD_B, axis=1)
    vroll = pltpu.roll(vband, HD_B, axis=1)
    kfirst = klane < HD_B
    kdup = [jnp.where(kfirst, kband, kroll).astype(BF16), jnp.where(kfirst, kroll, kband).astype(BF16)]
    vdup = [jnp.where(kfirst, vband, vroll).astype(BF16), jnp.where(kfirst, vroll, vband).astype(BF16)]
    group = H_B // KV_B
    outs = []
    for p in range(H_B // 2):
        sl = slice(p * LANES, (p + 1) * LANES)
        g = (2 * p) // group
        q = q_ref[:, sl] * (HD_B ** -0.5)
        o_pair = []
        for u, sel in enumerate((first, jnp.logical_not(first))):
            h = 2 * p + u
            s = _dot_nt(jnp.where(sel, q, 0.0).astype(BF16), kdup[g])
            s = jnp.where(valid, s - _alibi_slope(h) * distf, NEG)
            sink = sink_ref[h]
            m = jnp.maximum(jnp.max(s, axis=-1, keepdims=True), sink)
            e = jnp.exp(s - m)
            pr = e / (jnp.sum(e, axis=-1, keepdims=True) + jnp.exp(sink - m))
            o_pair.append(_dot(pr.astype(BF16), vdup[g]))
        o = jnp.where(first, o_pair[0], o_pair[1])
        outs.append(o * _silu(g_ref[:, sl]))
    o_ref[...] = jnp.concatenate(outs, axis=-1).astype(o_ref.dtype)


def swa_attn(sink, zq, q_col, g_col, k_prev, k_cur, v_prev, v_cur, n_batch, t, prompt, out_dtype):
    tq = WINDOW if prompt else t
    nq = t // tq
    kv_specs = [pl.BlockSpec((WINDOW, LANES) if a.ndim == 2 else (None, WINDOW, LANES), im)
                for a, im in (k_prev, k_cur, v_prev, v_cur)]
    return pl.pallas_call(
        functools.partial(_swa_kernel, prompt=prompt),
        out_shape=jax.ShapeDtypeStruct((n_batch * t, W_B), out_dtype),
        grid=(n_batch, nq),
        in_specs=[pl.BlockSpec(memory_space=pltpu.SMEM),
                  pl.BlockSpec((tq, W_B), lambda b, i: (b * nq + i, q_col)),
                  pl.BlockSpec((tq, W_B), lambda b, i: (b * nq + i, g_col))] + kv_specs,
        out_specs=pl.BlockSpec((tq, W_B), lambda b, i: (b * nq + i, 0)),
        compiler_params=_cparams(("parallel", "arbitrary")),
        name="swa_attn",
    )(sink, zq, zq, k_prev[0], k_cur[0], v_prev[0], v_cur[0])


def _fox_gates_kernel(f_ref, bf_ref, lf_ref, ct_ref, carry_sc, *, seg, carry):
    tb = f_ref.shape[0]
    x = f_ref[...] + bf_ref[...]
    lf = jnp.minimum(x, 0.0) - jnp.log1p(jnp.exp(-jnp.abs(x)))
    lf_ref[...] = lf[:, :H_C]
    s_i = lax.broadcasted_iota(jnp.int32, (tb, tb), 0)
    t_i = lax.broadcasted_iota(jnp.int32, (tb, tb), 1)
    ut = ((s_i <= t_i) & (s_i // seg == t_i // seg)).astype(F32)
    cum = _dot_hi(lf.T, ut)
    if carry:
        @pl.when(pl.program_id(1) == 0)
        def _():
            carry_sc[...] = jnp.zeros_like(carry_sc)
        cum = cum + carry_sc[...]
        carry_sc[...] = cum[:, tb - 1:tb]
    ct_ref[...] = cum[:H_C, :]


def fox_gates(z, f_col, bf, n_batch, t, seg):
    carry = seg == t
    tb = _largest_tile(t, 512, LANES) if carry else t
    assert carry or n_batch == 1
    nt = t // tb
    bf_pad = jnp.zeros((1, LANES), F32).at[0, :H_C].set(bf)
    return pl.pallas_call(
        functools.partial(_fox_gates_kernel, seg=seg, carry=carry),
        out_shape=(jax.ShapeDtypeStruct((n_batch * t, H_C), F32),
                   jax.ShapeDtypeStruct((n_batch, H_C, t), F32)),
        grid=(n_batch, nt),
        in_specs=[pl.BlockSpec((tb, LANES), lambda b, j: (b * nt + j, f_col)),
                  pl.BlockSpec((1, LANES), lambda b, j: (0, 0))],
        out_specs=(pl.BlockSpec((tb, H_C), lambda b, j: (b * nt + j, 0)),
                   pl.BlockSpec((None, H_C, tb), lambda b, j: (b, 0, j))),
        scratch_shapes=[pltpu.VMEM((LANES, 1), F32)],
        compiler_params=_cparams(("parallel", "arbitrary")),
        name="fox_gates",
    )(z, bf_pad)


FOX_TILE = 512


def _fox_prompt_kernel(q_ref, k_ref, v_ref, g_ref, ct_ref, o_ref, kb_sc, vb_sc):
    tq = q_ref.shape[0]
    h = pl.program_id(1)
    i = pl.program_id(2)

    @pl.when(i == 0)
    def _():
        kb_sc[...] = k_ref[...].astype(BF16)
        vb_sc[...] = v_ref[...].astype(BF16)

    q = (q_ref[...] * (HD_C ** -0.5)).astype(BF16)
    c_ref = ct_ref[pl.ds(h, 1), pl.ds(i, 1), :][0][:, 0:1]

    def chunk(j, carry, diagonal):
        m, l, acc = carry
        off = pl.multiple_of(j * tq, tq)
        kc = kb_sc[pl.ds(off, tq), :]
        vc = vb_sc[pl.ds(off, tq), :]
        bias = c_ref - ct_ref[pl.ds(h, 1), pl.ds(j, 1), :][0]
        s = _dot_nt(q, kc) + bias
        if diagonal:
            r_i = lax.broadcasted_iota(jnp.int32, (tq, tq), 0)
            c_i = lax.broadcasted_iota(jnp.int32, (tq, tq), 1)
            s = jnp.where(c_i <= r_i, s, NEG)
        m_new = jnp.maximum(m, jnp.max(s, axis=-1, keepdims=True))
        alpha = jnp.exp(m - m_new)
        p = jnp.exp(s - m_new)
        l = l * alpha + jnp.sum(p, axis=-1, keepdims=True)
        acc = acc * alpha + _dot(p.astype(BF16), vc)
        return m_new, l, acc

    init = (jnp.full((tq, 1), NEG, F32), jnp.zeros((tq, 1), F32), jnp.zeros((tq, HD_C), F32))
    carry = lax.fori_loop(0, i, lambda j, c: chunk(j, c, False), init)
    _, l, acc = chunk(i, carry, True)
    o_ref[...] = (acc / l * _silu(g_ref[...])).astype(o_ref.dtype)


def fox_prompt(q, k, v, gate, ct, n_batch, t):
    tq = FOX_TILE
    nq = t // tq
    ct4 = ct.reshape(n_batch, H_C, nq, tq)
    return pl.pallas_call(
        _fox_prompt_kernel,
        out_shape=jax.ShapeDtypeStruct((n_batch * t, W_C), BF16),
        grid=(n_batch, H_C, nq),
        in_specs=[pl.BlockSpec((tq, HD_C), lambda b, h, i: (b * nq + i, h)),
                  pl.BlockSpec((t, HD_C), lambda b, h, i: (b, h)),
                  pl.BlockSpec((t, HD_C), lambda b, h, i: (b, h)),
                  pl.BlockSpec((tq, HD_C), lambda b, h, i: (b * nq + i, h)),
                  pl.BlockSpec((None, H_C, nq, tq), lambda b, h, i: (b, 0, 0, 0))],
        out_specs=pl.BlockSpec((tq, HD_C), lambda b, h, i: (b * nq + i, h)),
        scratch_shapes=[pltpu.VMEM((t, HD_C), BF16), pltpu.VMEM((t, HD_C), BF16)],
        compiler_params=_cparams(("parallel", "parallel", "arbitrary")),
        name="fox_prompt",
    )(q, k, v, gate, ct4)


FOX_PAGES_PER_STEP = 4


def _fox_decode_kernel(pt_ref, q_ref, kn_ref, vn_ref, g_ref, cn_ref, *rest, n_new):
    pp = FOX_PAGES_PER_STEP
    k_refs, v_refs, lf_refs = rest[:pp], rest[pp:2 * pp], rest[2 * pp:3 * pp]
    o_ref = rest[3 * pp]
    q_sc, m_sc, l_sc, acc_sc, carry_sc = rest[3 * pp + 1:]
    step = pl.program_id(1)
    rows = H_C * n_new

    def new_head(ref, h):
        return ref[:, h * HD_C:(h + 1) * HD_C]

    def cached_head(ref, h):
        return ref[pl.ds(h, PAGE, stride=H_C), :]

    def attend(k_ref, v_ref, head, bias, mask):
        s_parts = []
        for h in range(H_C):
            kh = head(k_ref, h).astype(BF16)
            s_parts.append(_dot_nt(q_sc[h], kh) + bias[h:h + 1, :])
        s = jnp.concatenate(s_parts, axis=0)
        if mask is not None:
            s = jnp.where(mask, s, NEG)
        m_prev = m_sc[...]
        m_new = jnp.maximum(m_prev, jnp.max(s, axis=-1, keepdims=True))
        alpha = jnp.exp(m_prev - m_new)
        p = jnp.exp(s - m_new)
        m_sc[...] = m_new
        l_sc[...] = l_sc[...] * alpha + jnp.sum(p, axis=-1, keepdims=True)
        pv = []
        for h in range(H_C):
            ph = p[h * n_new:(h + 1) * n_new, :].astype(BF16)
            pv.append(_dot(ph, head(v_ref, h).astype(BF16)))
        acc_sc[...] = acc_sc[...] * alpha + jnp.concatenate(pv, axis=0)

    @pl.when(step == 0)
    def _():
        for h in range(H_C):
            q_sc[h] = (q_ref[:, h * HD_C:(h + 1) * HD_C] * (HD_C ** -0.5)).astype(BF16)
        m_sc[...] = jnp.full_like(m_sc, NEG)
        l_sc[...] = jnp.zeros_like(l_sc)
        acc_sc[...] = jnp.zeros_like(acc_sc)
        carry_sc[...] = jnp.zeros_like(carry_sc)
        r_i = lax.broadcasted_iota(jnp.int32, (rows, PAGE), 0)
        c_i = lax.broadcasted_iota(jnp.int32, (rows, PAGE), 1)
        attend(kn_ref, vn_ref, new_head, -cn_ref[...], c_i <= r_i % n_new)

    s_i = lax.broadcasted_iota(jnp.int32, (PAGE, PAGE), 0)
    p_i = lax.broadcasted_iota(jnp.int32, (PAGE, PAGE), 1)
    later = (s_i > p_i).astype(F32)
    for u in range(pp):
        lft = lf_refs[u][...]
        bias = _dot_hi(lft, later) + carry_sc[...]
        carry_sc[...] = carry_sc[...] + jnp.sum(lft, axis=-1, keepdims=True)
        attend(k_refs[u], v_refs[u], cached_head, bias, None)

    @pl.when(step == pl.num_programs(1) - 1)
    def _():
        o = acc_sc[...] / l_sc[...]
        o = jnp.concatenate([o[h * n_new:(h + 1) * n_new, :] for h in range(H_C)], axis=1)
        o_ref[...] = o * _silu(g_ref[...])


def fox_decode(q, k_new_pad, v_new_pad, gate, cnew_t, cache_k, cache_v, lft_pool, page_table):
    n_batch, n_new, _ = q.shape
    n_pages = page_table.shape[1]
    pp = FOX_PAGES_PER_STEP
    n_steps = n_pages // pp
    rows = H_C * n_new

    def page(u):
        return lambda b, i, pt: (pt[b * n_pages + n_pages - 1 - (i * pp + u)], 0, 0)

    per_batch = lambda b, i, pt: (b, 0, 0)
    in_specs = [pl.BlockSpec((None, n_new, W_C), per_batch),
                pl.BlockSpec((None, PAGE, W_C), per_batch),
                pl.BlockSpec((None, PAGE, W_C), per_batch),
                pl.BlockSpec((None, n_new, W_C), per_batch),
                pl.BlockSpec((None, H_C, PAGE), per_batch)]
    in_specs += [pl.BlockSpec((None, PAGE * H_C, HD_C), page(u)) for u in range(pp)]
    in_specs += [pl.BlockSpec((None, PAGE * H_C, HD_C), page(u)) for u in range(pp)]
    in_specs += [pl.BlockSpec((None, H_C, PAGE), page(u)) for u in range(pp)]
    grid_spec = pltpu.PrefetchScalarGridSpec(
        num_scalar_prefetch=1,
        grid=(n_batch, n_steps),
        in_specs=in_specs,
        out_specs=pl.BlockSpec((None, n_new, W_C), per_batch),
        scratch_shapes=[pltpu.VMEM((H_C, n_new, HD_C), BF16),
                        pltpu.VMEM((rows, 1), F32), pltpu.VMEM((rows, 1), F32),
                        pltpu.VMEM((rows, HD_C), F32), pltpu.VMEM((H_C, 1), F32)])
    return pl.pallas_call(
        functools.partial(_fox_decode_kernel, n_new=n_new),
        out_shape=jax.ShapeDtypeStruct((n_batch, n_new, W_C), F32),
        grid_spec=grid_spec,
        compiler_params=_cparams(("parallel", "arbitrary")),
        name="fox_decode",
    )(page_table.reshape(-1), q, k_new_pad, v_new_pad, gate, cnew_t,
      *([cache_k] * pp), *([cache_v] * pp), *([lft_pool] * pp))


RWKV_GROUP = 256


def _seg_sum(x, gmat):
    n = x.shape[0]
    parts = [x[:, c * RWKV_GROUP:(c + 1) * RWKV_GROUP] for c in range(W_A // RWKV_GROUP)]
    r = _dot_hi(jnp.concatenate(parts, axis=0), gmat)
    return jnp.concatenate([r[c * n:(c + 1) * n] for c in range(W_A // RWKV_GROUP)], axis=1)


def _rwkv_kernel(z_ref, sp_ref, s0_ref, mu_ref, w0_ref, a0_ref, lora_ref, kk_ref, ka_ref, rk_ref,
                 lng_ref, lnb_ref, o_ref, so_ref, state_sc, prev_sc):
    L = CHUNK
    n_in = z_ref.shape[0]
    c_idx = pl.program_id(1)

    @pl.when(c_idx == 0)
    def _():
        state_sc[...] = s0_ref[...]
        prev_sc[...] = sp_ref[...]

    z = z_ref[...]
    if n_in < L:
        z = jnp.concatenate([z, jnp.zeros((L - n_in, z.shape[1]), F32)], axis=0)
    cols = z[:, :SHIFT_W]
    gate = z[:, SHIFT_W:SHIFT_W + W_A]
    row = lax.broadcasted_iota(jnp.int32, (L, 1), 0)
    prev = jnp.where(row == 0, prev_sc[...], pltpu.roll(cols, 1, axis=0))
    prev_sc[...] = cols[n_in - 1:n_in, :]
    xs = cols + (prev - cols) * mu_ref[...]
    r = xs[:, 0:W_A]
    k = xs[:, W_A:2 * W_A]
    v = xs[:, 2 * W_A:3 * W_A]
    lo = xs[:, 3 * W_A:SHIFT_W]
    lane = lax.broadcasted_iota(jnp.int32, (L, LANES), 1)
    first = lane < N_A
    lo = jnp.where(lane < LORA, jnp.tanh(lo), lo)
    up = _dot_hi(lo, lora_ref[...])
    logw = -float(np.exp(-0.5)) * jax.nn.sigmoid(w0_ref[...] + up[:, :W_A])
    lr = jax.nn.sigmoid(a0_ref[...] + up[:, W_A:])

    g_i = lax.broadcasted_iota(jnp.int32, (RWKV_GROUP, RWKV_GROUP), 0)
    g_j = lax.broadcasted_iota(jnp.int32, (RWKV_GROUP, RWKV_GROUP), 1)
    gmat = (g_i // N_A == g_j // N_A).astype(F32)

    kk = k * kk_ref[...]
    kk = kk / jnp.maximum(jnp.sqrt(_seg_sum(kk * kk, gmat)), 1e-12)
    k = k * (1.0 + (lr - 1.0) * ka_ref[...])
    if n_in < L:
        real = row < n_in
        logw = jnp.where(real, logw, 0.0)
        kk = jnp.where(real, kk, 0.0)
        k = jnp.where(real, k, 0.0)

    t_i = lax.broadcasted_iota(jnp.int32, (L, L), 0)
    s_i = lax.broadcasted_iota(jnp.int32, (L, L), 1)
    cum = _dot_hi((s_i <= t_i).astype(F32), logw)
    cum_l = cum[L - 1:L, :]
    g_incl = jnp.exp(cum)
    g_inv = jnp.exp(-cum)
    g_tail = jnp.exp(cum_l - cum)
    abar = -kk * jnp.exp(cum - logw)
    b = kk * lr
    bbar = b * g_inv
    kbar = k * g_inv
    rbar = r * g_incl
    btail = b * g_tail
    ktail = k * g_tail
    g_l = jnp.exp(cum_l)

    n2 = 2 * L
    r_i = lax.broadcasted_iota(jnp.int32, (n2, n2), 0)
    c_i = lax.broadcasted_iota(jnp.int32, (n2, n2), 1)
    same = r_i // L == c_i // L
    strict = same & (c_i % L < r_i % L)
    incl = same & (c_i % L <= r_i % L)
    eye = (r_i == c_i).astype(F32)

    def stack(x, sl):
        xp = x[:, sl]
        return jnp.concatenate([jnp.where(first, xp, 0.0), jnp.where(first, 0.0, xp)], axis=0)

    ys = []
    for p in range(H_A // 2):
        sl = slice(p * LANES, (p + 1) * LANES)
        xa, xb, xk, xr = stack(abar, sl), stack(bbar, sl), stack(kbar, sl), stack(rbar, sl)
        vst = stack(v, sl)
        big = _dot_nt(jnp.concatenate([xa, xr], axis=0).astype(BF16),
                      jnp.concatenate([xb, xk], axis=0).astype(BF16))
        a_ab = jnp.where(strict, big[:n2, :n2], 0.0)
        a_ak = jnp.where(strict, big[:n2, n2:], 0.0)
        a_rb = jnp.where(incl, big[n2:, :n2], 0.0)
        a_rk = jnp.where(incl, big[n2:, n2:], 0.0)
        tinv = eye + a_ab
        pw = a_ab
        for _ in range(int(np.log2(L)) - 1):
            pw = _dot(pw.astype(BF16), pw.astype(BF16))
            tinv = tinv + _dot(tinv.astype(BF16), pw.astype(BF16))
        s_old = state_sc[p]
        s_bf = s_old.astype(BF16)
        vst_bf = vst.astype(BF16)
        rhs = _dot_nt(xa.astype(BF16), s_bf) + _dot(a_ak.astype(BF16), vst_bf)
        u = _dot(tinv.astype(BF16), rhs.astype(BF16))
        u_bf = u.astype(BF16)
        yst = _dot_nt(xr.astype(BF16), s_bf) + _dot(a_rb.astype(BF16), u_bf) + _dot(a_rk.astype(BF16), vst_bf)
        ys.append(yst[:L] + yst[L:])
        upd = _dot_tn(jnp.concatenate([u_bf, vst_bf], axis=0),
                      jnp.concatenate([stack(btail, sl), stack(ktail, sl)], axis=0).astype(BF16))
        state_sc[p] = s_old * g_l[:, sl] + upd
    y = jnp.concatenate(ys, axis=1)

    mean = _seg_sum(y, gmat) * (1.0 / N_A)
    dev = y - mean
    var = _seg_sum(dev * dev, gmat) * (1.0 / N_A)
    yn = dev * lax.rsqrt(var + GN_EPS) * lng_ref[...] + lnb_ref[...]
    bonus = _seg_sum(r * k * rk_ref[...], gmat) * v
    out = (yn + bonus) * _silu(gate)
    o_ref[...] = out[:n_in].astype(o_ref.dtype)
    so_ref[...] = state_sc[...]


def rwkv7(z, shift_prev, s0_blk, mu, w0, w_up, a0, a_up, k_k, k_a, r_k, ln_g, ln_b, n_batch, t, out_dtype):
    n_in = min(t, CHUNK)
    nc = t // n_in
    lora = jnp.zeros((2 * LORA, 2 * W_A), F32).at[:LORA, :W_A].set(w_up).at[LORA:, W_A:].set(a_up)
    vec = lambda a, n: a.reshape(1, n)
    const = lambda shape: pl.BlockSpec(shape, lambda b, c: (0,) * len(shape))
    return pl.pallas_call(
        _rwkv_kernel,
        out_shape=(jax.ShapeDtypeStruct((n_batch * t, W_A), out_dtype),
                   jax.ShapeDtypeStruct(s0_blk.shape, F32)),
        grid=(n_batch, nc),
        in_specs=[pl.BlockSpec((n_in, SHIFT_W + W_A), lambda b, c: (b * nc + c, 0)),
                  pl.BlockSpec((None, 1, SHIFT_W), lambda b, c: (b, 0, 0)),
                  pl.BlockSpec((None, H_A // 2, LANES, LANES), lambda b, c: (b, 0, 0, 0)),
                  const((1, SHIFT_W)), const((1, W_A)), const((1, W_A)), const((2 * LORA, 2 * W_A)),
                  const((1, W_A)), const((1, W_A)), const((1, W_A)), const((1, W_A)), const((1, W_A))],
        out_specs=(pl.BlockSpec((n_in, W_A), lambda b, c: (b * nc + c, 0)),
                   pl.BlockSpec((None, H_A // 2, LANES, LANES), lambda b, c: (b, 0, 0, 0))),
        scratch_shapes=[pltpu.VMEM((H_A // 2, LANES, LANES), F32), pltpu.VMEM((1, SHIFT_W), F32)],
        compiler_params=_cparams(("parallel", "arbitrary")),
        name="rwkv7",
    )(z, shift_prev.reshape(n_batch, 1, SHIFT_W), s0_blk, vec(mu, SHIFT_W), vec(w0, W_A), vec(a0, W_A), lora,
      vec(k_k, W_A), vec(k_a, W_A), vec(r_k, W_A), vec(ln_g, W_A), vec(ln_b, W_A))


def _pair_blockdiag(s):
    b = s.shape[0]
    s = s.reshape(b, H_A // 2, 2, N_A, N_A)
    z = jnp.zeros_like(s[:, :, 0])
    top = jnp.concatenate([s[:, :, 0], z], axis=-1)
    bot = jnp.concatenate([z, s[:, :, 1]], axis=-1)
    return jnp.concatenate([top, bot], axis=-2)


def _pair_diag_blocks(sb):
    b = sb.shape[0]
    h0 = sb[:, :, :N_A, :N_A]
    h1 = sb[:, :, N_A:, N_A:]
    return jnp.stack([h0, h1], axis=2).reshape(b, H_A, N_A, N_A)


_E_SIZES = (SHIFT_W, W_A, W_B, KV_B * HD_B, KV_B * HD_B, W_B, W_M, W_M)
_E_OFF = np.concatenate([[0], np.cumsum(_E_SIZES)])
_O_SIZES = (W_C, W_C, W_C, H_C, W_C, W_M, W_M)
_O_OFF = np.concatenate([[0], np.cumsum(_O_SIZES)])
_SWA_K_BLOCK = 2 * W_B // LANES
_SWA_V_BLOCK = _SWA_K_BLOCK + 1
_FM_F_BLOCK = 2 * W_M // LANES


def _ecol(w, i, j=None):
    return w[:, int(_E_OFF[i]):int(_E_OFF[(i if j is None else j) + 1])]


def _ocol(w, i, j=None):
    return w[:, int(_O_OFF[i]):int(_O_OFF[(i if j is None else j) + 1])]


def _even_layer(x2d, n_batch, t, mk, mv, shift_prev, s0_blk, swa_cache, w, out_dtype):
    xn = rmsnorm_cast(x2d, w["norm_e"])
    z_r = matmul(xn, w["w_rwkv"])
    z_s = matmul(xn, w["w_swa"])
    z_m = matmul(xn, w["w_mem0"])
    a_out, s_blk = rwkv7(z_r, shift_prev, s0_blk, *w["rwkv"], n_batch, t, out_dtype)
    new_shift = z_r.reshape(n_batch, t, -1)[:, -1, :SHIFT_W]
    k_new = z_s[:, 2 * W_B:2 * W_B + LANES].reshape(n_batch, t, LANES)
    v_new = z_s[:, 2 * W_B + LANES:].reshape(n_batch, t, LANES)
    if swa_cache is None:
        nq = t // WINDOW
        cur = lambda c: (lambda b, i: (b * nq + i, c))
        prev = lambda c: (lambda b, i: (b * nq + jnp.maximum(i - 1, 0), c))
        b_out = swa_attn(w["sink"], z_s, 0, 1, (z_s, prev(_SWA_K_BLOCK)), (z_s, cur(_SWA_K_BLOCK)),
                         (z_s, prev(_SWA_V_BLOCK)), (z_s, cur(_SWA_V_BLOCK)), n_batch, t, True, out_dtype)
        k_win, v_win = k_new[:, -WINDOW:], v_new[:, -WINDOW:]
    else:
        k_buf, v_buf = (c.reshape(n_batch, WINDOW, LANES) for c in swa_cache)
        pad = lambda a: jnp.pad(a, ((0, 0), (0, WINDOW - t), (0, 0)))
        per_b = lambda b, i: (b, 0, 0)
        b_out = swa_attn(w["sink"], z_s, 0, 1, (k_buf, per_b), (pad(k_new), per_b),
                         (v_buf, per_b), (pad(v_new), per_b), n_batch, t, False, out_dtype)
        k_win = jnp.concatenate([k_buf, k_new], axis=1)[:, -WINDOW:]
        v_win = jnp.concatenate([v_buf, v_new], axis=1)[:, -WINDOW:]
    m_out = mem_attn(z_m, 0, 1, mk, mv, n_batch, t, out_dtype)
    x1, xn1 = outproj([a_out, b_out, m_out], w["w_out_e"], x2d, w["norm_o"], False)
    kv_shape = (n_batch, WINDOW, KV_B, HD_B)
    return x1, xn1, (new_shift, _pair_diag_blocks(s_blk), k_win.reshape(kv_shape), v_win.reshape(kv_shape))


def _odd_layer(x1, xn1, n_batch, t, mk, mv, fox_past, w, out_dtype):
    q = matmul(xn1, w["w_q"])
    k = matmul(xn1, w["w_k"])
    v = matmul(xn1, w["w_v"])
    gate = matmul(xn1, w["w_gc"])
    z_fm = matmul(xn1, w["w_fm"])
    if fox_past is None:
        logf, ct = fox_gates(z_fm, _FM_F_BLOCK, w["fox_bf"], n_batch, t, t)
        c_out = fox_prompt(q, k, v, gate, ct, n_batch, t)
    else:
        cache_k, cache_v, lft_pool, page_table = fox_past
        logf, cs = fox_gates(z_fm, _FM_F_BLOCK, w["fox_bf"], 1, n_batch * t, t)
        cnew = jnp.moveaxis(cs.reshape(H_C, n_batch, t), 1, 0)
        cnew = jnp.pad(cnew, ((0, 0), (0, 0), (0, PAGE - t)))
        r3 = lambda a: a.reshape(n_batch, t, W_C)
        pad = lambda a: jnp.pad(r3(a), ((0, 0), (0, PAGE - t), (0, 0)))
        c_out = fox_decode(r3(q), pad(k), pad(v), r3(gate), cnew, cache_k, cache_v, lft_pool, page_table)
        c_out = c_out.reshape(n_batch * t, W_C)
    m_out = mem_attn(z_fm, 0, 1, mk, mv, n_batch, t, out_dtype)
    y = outproj([c_out, m_out], w["w_out_o"], x1, w["norm_f"], True)
    kv_shape = (n_batch, t, H_C, HD_C)
    return y, (k.reshape(kv_shape), v.reshape(kv_shape), logf.reshape(n_batch, t, H_C))


def kernel(x_prompt, x_sample, state_rwkv, state_shift, cache_swa_k, cache_swa_v, cache_mem_k, cache_mem_v, cache_fox_k, cache_fox_v, cache_fox_logf, page_table, mem_prompt, norm_e, w_in_e, rwkv_mu, rwkv_w0, rwkv_w_up, rwkv_a0, rwkv_a_up, rwkv_kk, rwkv_ka, rwkv_rk, rwkv_ln_g, rwkv_ln_b, swa_sink, w_out_e, norm_o, w_in_o, fox_bf, w_out_o, mem_norm, w_mem_k, w_mem_v, norm_f):
    n_b, t, d = x_prompt.shape
    n_db, s, _ = x_sample.shape
    assert state_rwkv.shape[0] == 1 and cache_fox_k.shape[0] == 1 and mem_norm.shape[0] == 2
    bf = lambda a: a.astype(BF16)
    we, wo = w_in_e[0], w_in_o[0]
    woe, woo = bf(w_out_e[0]), bf(w_out_o[0])
    w = {
        "norm_e": norm_e[0], "norm_o": norm_o[0], "norm_f": norm_f, "sink": swa_sink[0], "fox_bf": fox_bf[0],
        "w_rwkv": bf(_ecol(we, 0, 1)),
        "w_swa": bf(jnp.concatenate([_ecol(we, 2), _ecol(we, 5), _ecol(we, 3, 4)], axis=1)),
        "w_mem0": bf(_ecol(we, 6, 7)),
        "rwkv": (rwkv_mu[0], rwkv_w0[0], rwkv_w_up[0], rwkv_a0[0], rwkv_a_up[0], rwkv_kk[0], rwkv_ka[0],
                 rwkv_rk[0].reshape(-1), rwkv_ln_g[0], rwkv_ln_b[0]),
        "w_out_e": [woe[:W_A], woe[W_A:W_A + W_B], woe[W_A + W_B:]],
        "w_q": bf(_ocol(wo, 0)), "w_k": bf(_ocol(wo, 1)), "w_v": bf(_ocol(wo, 2)), "w_gc": bf(_ocol(wo, 4)),
        "w_fm": bf(jnp.concatenate([_ocol(wo, 5, 6), _ocol(wo, 3), jnp.zeros((d, LANES - H_C), F32)], axis=1)),
        "w_out_o": [woo[:W_C], woo[W_C:]],
    }

    p_mem = []
    for layer in range(2):
        hm = rmsnorm_cast(mem_prompt.reshape(n_b * N_MEM, d), mem_norm[layer])
        kv = matmul(hm, bf(jnp.concatenate([w_mem_k[layer], w_mem_v[layer]], axis=1)))
        p_mem.append((kv[:, :W_M].reshape(n_b, N_MEM, W_M), kv[:, W_M:].reshape(n_b, N_MEM, W_M)))
    xp = x_prompt.reshape(n_b * t, d)
    xp1, xpn1, (p_shift, p_rwkv, p_swa_k, p_swa_v) = _even_layer(
        xp, n_b, t, p_mem[0][0], p_mem[0][1], jnp.zeros((n_b, SHIFT_W), F32),
        jnp.zeros((n_b, H_A // 2, LANES, LANES), F32), None, w, BF16)
    y_prompt, (p_fox_k, p_fox_v, p_fox_lf) = _odd_layer(xp1, xpn1, n_b, t, p_mem[1][0], p_mem[1][1], None, w, BF16)

    xs = x_sample.reshape(n_db * s, d)
    memc = lambda c, layer: c[layer].reshape(n_db, N_MEM, W_M)
    xs1, xsn1, (s_shift, s_rwkv, s_swa_k, s_swa_v) = _even_layer(
        xs, n_db, s, memc(cache_mem_k, 0), memc(cache_mem_v, 0), state_shift[0], _pair_blockdiag(state_rwkv[0]),
        (cache_swa_k[0], cache_swa_v[0]), w, F32)
    n_pool = cache_fox_k.shape[1]
    fox_past = (cache_fox_k.reshape(n_pool, PAGE * H_C, HD_C), cache_fox_v.reshape(n_pool, PAGE * H_C, HD_C),
                jnp.swapaxes(cache_fox_logf[0], 1, 2), page_table)
    y_sample, (s_fox_k, s_fox_v, s_fox_lf) = _odd_layer(xs1, xsn1, n_db, s, memc(cache_mem_k, 1), memc(cache_mem_v, 1),
                                                       fox_past, w, F32)

    mem_shape = (n_b, N_MEM, H_M, HD_M)
    return (y_prompt.reshape(n_b, t, d), y_sample.reshape(n_db, s, d),
            p_rwkv[None], p_shift[None], p_swa_k[None], p_swa_v[None],
            jnp.stack([p_mem[0][0].reshape(mem_shape), p_mem[1][0].reshape(mem_shape)]),
            jnp.stack([p_mem[0][1].reshape(mem_shape), p_mem[1][1].reshape(mem_shape)]),
            p_fox_k[None], p_fox_v[None], p_fox_lf[None],
            s_rwkv[None], s_shift[None], s_swa_k[None], s_swa_v[None],
            s_fox_k[None], s_fox_v[None], s_fox_lf[None])
```

```python
import functools

import jax
import jax.numpy as jnp
import numpy as np
from jax import lax
from jax.experimental import pallas as pl
from jax.experimental.pallas import tpu as pltpu

F32 = jnp.float32
BF16 = jnp.bfloat16
HIGHEST = lax.Precision.HIGHEST

D_MODEL = 2048
EPS = 1e-6
NEG = -1e30
H_A, N_A = 16, 64
W_A = H_A * N_A
LORA = 64
SHIFT_W = 3 * W_A + 2 * LORA
GN_EPS = 64e-5
CHUNK = 64
H_B, KV_B, HD_B = 16, 2, 64
W_B = H_B * HD_B
WINDOW = 128
H_C, HD_C = 16, 128
W_C = H_C * HD_C
PAGE = 128
N_MEM, H_M, HD_M = 256, 4, 64
W_M = H_M * HD_M

LANES = 128
VMEM_LIMIT = 48 * 1024 * 1024


def _cparams(sem):
    return pltpu.CompilerParams(dimension_semantics=sem, vmem_limit_bytes=VMEM_LIMIT)


def _dot(a, b):
    return jnp.dot(a, b, preferred_element_type=F32)


def _dot_nt(a, b):
    return lax.dot_general(a, b, (((1,), (1,)), ((), ())), preferred_element_type=F32)


def _dot_tn(a, b):
    return lax.dot_general(a, b, (((0,), (0,)), ((), ())), preferred_element_type=F32)


def _dot_hi(a, b):
    return jnp.dot(a, b, preferred_element_type=F32, precision=HIGHEST)


def _silu(g):
    return g * jax.nn.sigmoid(g)


def _largest_tile(n, cap, mult):
    best = None
    for t in range(mult, min(n, cap) + 1, mult):
        if n % t == 0:
            best = t
    assert best is not None, (n, cap, mult)
    return best


def _rmsnorm_kernel(x_ref, g_ref, o_ref):
    x = x_ref[...]
    ms = jnp.mean(x * x, axis=-1, keepdims=True)
    o_ref[...] = (x * lax.rsqrt(ms + EPS) * g_ref[...]).astype(o_ref.dtype)


def rmsnorm_cast(x, g):
    m, d = x.shape
    tm = _largest_tile(m, 512, 8)
    return pl.pallas_call(
        _rmsnorm_kernel,
        out_shape=jax.ShapeDtypeStruct((m, d), BF16),
        grid=(m // tm,),
        in_specs=[pl.BlockSpec((tm, d), lambda i: (i, 0)),
                  pl.BlockSpec((1, d), lambda i: (0, 0))],
        out_specs=pl.BlockSpec((tm, d), lambda i: (i, 0)),
        compiler_params=_cparams(("parallel",)),
        name="rmsnorm_cast",
    )(x, g.reshape(1, d))


def _matmul_kernel(x_ref, w_ref, o_ref):
    o_ref[...] = _dot(x_ref[...], w_ref[...])


def matmul(x, w):
    m, k = x.shape
    n = w.shape[1]
    tm = _largest_tile(m, 1024, 8)
    tn = _largest_tile(n, 1536, LANES)
    return pl.pallas_call(
        _matmul_kernel,
        out_shape=jax.ShapeDtypeStruct((m, n), F32),
        grid=(m // tm, n // tn),
        in_specs=[pl.BlockSpec((tm, k), lambda i, j: (i, 0)),
                  pl.BlockSpec((k, tn), lambda i, j: (0, j))],
        out_specs=pl.BlockSpec((tm, tn), lambda i, j: (i, j)),
        compiler_params=_cparams(("parallel", "parallel")),
        name="matmul",
    )(x, w)


def _outproj_kernel(*refs, n_in, final):
    hs = refs[:n_in]
    ws = refs[n_in:2 * n_in]
    x_ref, g_ref = refs[2 * n_in], refs[2 * n_in + 1]
    outs = refs[2 * n_in + 2:]
    acc = x_ref[...]
    for h_ref, w_ref in zip(hs, ws):
        acc = acc + _dot(h_ref[...].astype(BF16), w_ref[...])
    ms = jnp.mean(acc * acc, axis=-1, keepdims=True)
    normed = acc * lax.rsqrt(ms + EPS) * g_ref[...]
    if final:
        outs[0][...] = normed
    else:
        outs[0][...] = acc
        outs[1][...] = normed.astype(BF16)


def outproj(hs, ws, x, g, final):
    m, d = x.shape
    tm = _largest_tile(m, 256, 16)
    n_in = len(hs)
    in_specs = [pl.BlockSpec((tm, h.shape[1]), lambda i: (i, 0)) for h in hs]
    in_specs += [pl.BlockSpec(w.shape, lambda i: (0, 0)) for w in ws]
    in_specs += [pl.BlockSpec((tm, d), lambda i: (i, 0)), pl.BlockSpec((1, d), lambda i: (0, 0))]
    row = pl.BlockSpec((tm, d), lambda i: (i, 0))
    if final:
        out_shape, out_specs = jax.ShapeDtypeStruct((m, d), F32), row
    else:
        out_shape = (jax.ShapeDtypeStruct((m, d), F32), jax.ShapeDtypeStruct((m, d), BF16))
        out_specs = (row, row)
    return pl.pallas_call(
        functools.partial(_outproj_kernel, n_in=n_in, final=final),
        out_shape=out_shape,
        grid=(m // tm,),
        in_specs=in_specs,
        out_specs=out_specs,
        compiler_params=_cparams(("parallel",)),
        name="outproj",
    )(*hs, *ws, x, g.reshape(1, d))


def _mem_attn_kernel(q_ref, g_ref, k_ref, v_ref, o_ref):
    tq = q_ref.shape[0]
    lane = lax.broadcasted_iota(jnp.int32, (tq, LANES), 1)
    first = lane < HD_M
    heads = range(H_M)
    sls = [slice((h // 2) * LANES, (h // 2 + 1) * LANES) for h in heads]
    sel = [first if h % 2 == 0 else jnp.logical_not(first) for h in heads]
    qs = [jnp.where(sel[h], q_ref[:, sls[h]] * (HD_M ** -0.5), 0.0).astype(BF16) for h in heads]
    kp = [k_ref[:, sls[h]].astype(BF16) for h in heads]
    vp = [v_ref[:, sls[h]].astype(BF16) for h in heads]
    s = [_dot_nt(qs[h], kp[h]) for h in heads]
    m = [jnp.max(s[h], axis=-1, keepdims=True) for h in heads]
    e = [jnp.exp(s[h] - m[h]) for h in heads]
    pr = [e[h] / jnp.sum(e[h], axis=-1, keepdims=True) for h in heads]
    o = [_dot(pr[h].astype(BF16), vp[h]) for h in heads]
    outs = [jnp.where(first, o[2 * p], o[2 * p + 1]) * _silu(g_ref[:, p * LANES:(p + 1) * LANES])
            for p in range(H_M // 2)]
    o_ref[...] = jnp.concatenate(outs, axis=-1).astype(o_ref.dtype)


def mem_attn(z, q_col, g_col, mk, mv, n_batch, t, out_dtype):
    tq = _largest_tile(t, 512, 8)
    nq = t // tq
    return pl.pallas_call(
        _mem_attn_kernel,
        out_shape=jax.ShapeDtypeStruct((n_batch * t, W_M), out_dtype),
        grid=(n_batch, nq),
        in_specs=[pl.BlockSpec((tq, W_M), lambda b, i: (b * nq + i, q_col)),
                  pl.BlockSpec((tq, W_M), lambda b, i: (b * nq + i, g_col)),
                  pl.BlockSpec((None, N_MEM, W_M), lambda b, i: (b, 0, 0)),
                  pl.BlockSpec((None, N_MEM, W_M), lambda b, i: (b, 0, 0))],
        out_specs=pl.BlockSpec((tq, W_M), lambda b, i: (b * nq + i, 0)),
        compiler_params=_cparams(("parallel", "parallel")),
        name="mem_attn",
    )(z, z, mk, mv)


def _alibi_slope(h):
    return float(2.0 ** (-8.0 * (h + 1) / H_B))


def _swa_kernel(sink_ref, q_ref, g_ref, kp_ref, kc_ref, vp_ref, vc_ref, o_ref, *, prompt):
    tq = q_ref.shape[0]
    nk = 2 * WINDOW
    lane = lax.broadcasted_iota(jnp.int32, (tq, LANES), 1)
    first = lane < HD_B
    klane = lax.broadcasted_iota(jnp.int32, (nk, LANES), 1)
    qi = lax.broadcasted_iota(jnp.int32, (tq, nk), 0)
    kj = lax.broadcasted_iota(jnp.int32, (tq, nk), 1)
    dist = qi + WINDOW - kj
    valid = (dist >= 0) & (dist < WINDOW)
    if prompt:
        has_prev = pl.program_id(1) > 0
        valid = valid & (has_prev | (kj >= WINDOW))
    distf = dist.astype(F32)
    kband = jnp.concatenate([kp_ref[...], kc_ref[...]], axis=0)
    vband = jnp.concatenate([vp_ref[...], vc_ref[...]], axis=0)
    kroll = pltpu.roll(kband, HD_B, axis=1)
    vroll = pltpu.roll(vband, HD_B, axis=1)
    kfirst = klane < HD_B
    kdup = [jnp.where(kfirst, kband, kroll).astype(BF16), jnp.where(kfirst, kroll, kband).astype(BF16)]
    vdup = [jnp.where(kfirst, vband, vroll).astype(BF16), jnp.where(kfirst, vroll, vband).astype(BF16)]
    group = H_B // KV_B
    heads = range(H_B)
    qs = [q_ref[:, (h // 2) * LANES:(h // 2 + 1) * LANES] * (HD_B ** -0.5) for h in heads]
    qs = [jnp.where(first if h % 2 == 0 else jnp.logical_not(first), qs[h], 0.0).astype(BF16) for h in heads]
    s = [_dot_nt(qs[h], kdup[h // group]) for h in heads]
    s = [jnp.where(valid, s[h] - _alibi_slope(h) * distf, NEG) for h in heads]
    sink = [sink_ref[h] for h in heads]
    m = [jnp.maximum(jnp.max(s[h], axis=-1, keepdims=True), sink[h]) for h in heads]
    e = [jnp.exp(s[h] - m[h]) for h in heads]
    pr = [e[h] / (jnp.sum(e[h], axis=-1, keepdims=True) + jnp.exp(sink[h] - m[h])) for h in heads]
    o = [_dot(pr[h].astype(BF16), vdup[h // group]) for h in heads]
    outs = [jnp.where(first, o[2 * p], o[2 * p + 1]) * _silu(g_ref[:, p * LANES:(p + 1) * LANES])
            for p in range(H_B // 2)]
    o_ref[...] = jnp.concatenate(outs, axis=-1).astype(o_ref.dtype)


def swa_attn(sink, zq, q_col, g_col, k_prev, k_cur, v_prev, v_cur, n_batch, t, prompt, out_dtype):
    tq = WINDOW if prompt else t
    nq = t // tq
    kv_specs = [pl.BlockSpec((WINDOW, LANES) if a.ndim == 2 else (None, WINDOW, LANES), im)
                for a, im in (k_prev, k_cur, v_prev, v_cur)]
    return pl.pallas_call(
        functools.partial(_swa_kernel, prompt=prompt),
        out_shape=jax.ShapeDtypeStruct((n_batch * t, W_B), out_dtype),
        grid=(n_batch, nq),
        in_specs=[pl.BlockSpec(memory_space=pltpu.SMEM),
                  pl.BlockSpec((tq, W_B), lambda b, i: (b * nq + i, q_col)),
                  pl.BlockSpec((tq, W_B), lambda b, i: (b * nq + i, g_col))] + kv_specs,
        out_specs=pl.BlockSpec((tq, W_B), lambda b, i: (b * nq + i, 0)),
        compiler_params=_cparams(("parallel", "arbitrary")),
        name="swa_attn",
    )(sink, zq, zq, k_prev[0], k_cur[0], v_prev[0], v_cur[0])


def _fox_gates_kernel(f_ref, bf_ref, lf_ref, ct_ref, carry_sc, *, seg, carry):
    tb = f_ref.shape[0]
    x = f_ref[...] + bf_ref[...]
    lf = jnp.minimum(x, 0.0) - jnp.log1p(jnp.exp(-jnp.abs(x)))
    lf_ref[...] = lf[:, :H_C]
    s_i = lax.broadcasted_iota(jnp.int32, (tb, tb), 0)
    t_i = lax.broadcasted_iota(jnp.int32, (tb, tb), 1)
    if carry:
        cum = _dot_hi(lf.T, (s_i <= t_i).astype(F32))

        @pl.when(pl.program_id(1) == 0)
        def _():
            carry_sc[...] = jnp.zeros_like(carry_sc)
        cum = cum + carry_sc[...]
        carry_sc[...] = cum[:, tb - 1:tb]
        ct_ref[...] = cum[:H_C, :]
    else:
        cum = _dot_hi(((t_i <= s_i) & (s_i // seg == t_i // seg)).astype(F32), lf)
        ct_ref[...] = cum[:, :H_C]


def fox_gates(z, f_col, bf, n_batch, t, seg):
    carry = seg == t
    tb = _largest_tile(t, 512, LANES) if carry else t
    assert carry or n_batch == 1
    nt = t // tb
    bf_pad = jnp.zeros((1, LANES), F32).at[0, :H_C].set(bf)
    if carry:
        c_shape = jax.ShapeDtypeStruct((n_batch, H_C, t), F32)
        c_spec = pl.BlockSpec((None, H_C, tb), lambda b, j: (b, 0, j))
    else:
        c_shape = jax.ShapeDtypeStruct((n_batch * t, H_C), F32)
        c_spec = pl.BlockSpec((tb, H_C), lambda b, j: (b * nt + j, 0))
    return pl.pallas_call(
        functools.partial(_fox_gates_kernel, seg=seg, carry=carry),
        out_shape=(jax.ShapeDtypeStruct((n_batch * t, H_C), F32), c_shape),
        grid=(n_batch, nt),
        in_specs=[pl.BlockSpec((tb, LANES), lambda b, j: (b * nt + j, f_col)),
                  pl.BlockSpec((1, LANES), lambda b, j: (0, 0))],
        out_specs=(pl.BlockSpec((tb, H_C), lambda b, j: (b * nt + j, 0)), c_spec),
        scratch_shapes=[pltpu.VMEM((LANES, 1), F32)],
        compiler_params=_cparams(("parallel", "arbitrary")),
        name="fox_gates",
    )(z, bf_pad)


FOX_TILE = 512
FOX_HEADS_PER_STEP = 2


def _fox_prompt_kernel(q_ref, k_ref, v_ref, g_ref, ct_ref, o_ref, kb_sc, vb_sc):
    tq = q_ref.shape[0]
    nh = FOX_HEADS_PER_STEP
    h0 = pl.program_id(1) * nh
    i = pl.program_id(2)

    @pl.when(i == 0)
    def _():
        kb_sc[...] = k_ref[...].astype(BF16)
        vb_sc[...] = v_ref[...].astype(BF16)

    lanes = [slice(u * HD_C, (u + 1) * HD_C) for u in range(nh)]
    q = [(q_ref[:, sl] * (HD_C ** -0.5)).astype(BF16) for sl in lanes]
    c_ref = [ct_ref[pl.ds(h0 + u, 1), pl.ds(i, 1), :][0][:, 0:1] for u in range(nh)]

    def chunk(j, carry, diagonal):
        off = pl.multiple_of(j * tq, tq)
        hs = range(nh)
        bias = [c_ref[u] - ct_ref[pl.ds(h0 + u, 1), pl.ds(j, 1), :][0] for u in hs]
        s = [_dot_nt(q[u], kb_sc[pl.ds(off, tq), lanes[u]]) + bias[u] for u in hs]
        if diagonal:
            r_i = lax.broadcasted_iota(jnp.int32, (tq, tq), 0)
            c_i = lax.broadcasted_iota(jnp.int32, (tq, tq), 1)
            s = [jnp.where(c_i <= r_i, x, NEG) for x in s]
        m_new = [jnp.maximum(carry[u][0], jnp.max(s[u], axis=-1, keepdims=True)) for u in hs]
        alpha = [jnp.exp(carry[u][0] - m_new[u]) for u in hs]
        p = [jnp.exp(s[u] - m_new[u]) for u in hs]
        l = [carry[u][1] * alpha[u] + jnp.sum(p[u], axis=-1, keepdims=True) for u in hs]
        pv = [_dot(p[u].astype(BF16), vb_sc[pl.ds(off, tq), lanes[u]]) for u in hs]
        return tuple((m_new[u], l[u], carry[u][2] * alpha[u] + pv[u]) for u in hs)

    init = tuple((jnp.full((tq, 1), NEG, F32), jnp.zeros((tq, 1), F32), jnp.zeros((tq, HD_C), F32))
                 for _ in range(nh))
    carry = lax.fori_loop(0, i, lambda j, c: chunk(j, c, False), init)
    carry = chunk(i, carry, True)
    for u in range(nh):
        _, l, acc = carry[u]
        o_ref[:, lanes[u]] = (acc / l * _silu(g_ref[:, lanes[u]])).astype(o_ref.dtype)


def fox_prompt(q, k, v, gate, ct, n_batch, t):
    tq = FOX_TILE
    nq = t // tq
    wide = FOX_HEADS_PER_STEP * HD_C
    ct4 = ct.reshape(n_batch, H_C, nq, tq)
    return pl.pallas_call(
        _fox_prompt_kernel,
        out_shape=jax.ShapeDtypeStruct((n_batch * t, W_C), BF16),
        grid=(n_batch, H_C // FOX_HEADS_PER_STEP, nq),
        in_specs=[pl.BlockSpec((tq, wide), lambda b, h, i: (b * nq + i, h)),
                  pl.BlockSpec((t, wide), lambda b, h, i: (b, h)),
                  pl.BlockSpec((t, wide), lambda b, h, i: (b, h)),
                  pl.BlockSpec((tq, wide), lambda b, h, i: (b * nq + i, h)),
                  pl.BlockSpec((None, H_C, nq, tq), lambda b, h, i: (b, 0, 0, 0))],
        out_specs=pl.BlockSpec((tq, wide), lambda b, h, i: (b * nq + i, h)),
        scratch_shapes=[pltpu.VMEM((t, wide), BF16), pltpu.VMEM((t, wide), BF16)],
        compiler_params=_cparams(("parallel", "parallel", "arbitrary")),
        name="fox_prompt",
    )(q, k, v, gate, ct4)


FOX_PAGES_PER_STEP = 8
FOX_POS_PER_VREG = LANES // H_C


def _split3(x):
    hi = x.astype(BF16)
    r1 = x - hi.astype(F32)
    mid = r1.astype(BF16)
    lo = (r1 - mid.astype(F32)).astype(BF16)
    return hi, mid, lo


def _head_allreduce(x, op):
    shift = H_C
    while shift < LANES:
        x = op(x, pltpu.roll(x, shift, axis=1))
        shift *= 2
    return x


def _fox_decode_kernel(pt_ref, q_ref, kn_ref, vn_ref, g_ref, cn_ref, *rest, n_new):
    pp = FOX_PAGES_PER_STEP
    k_refs, v_refs, lf_refs = rest[:pp], rest[pp:2 * pp], rest[2 * pp:3 * pp]
    o_ref = rest[3 * pp]
    q_sc, m_sc, l_sc, acc_sc, carry_sc = rest[3 * pp + 1:]
    step = pl.program_id(1)
    lane = lax.broadcasted_iota(jnp.int32, (n_new, LANES), 1)
    head_of_lane = lane % H_C
    is_head = [head_of_lane == h for h in range(H_C)]

    def tree(op, xs):
        while len(xs) > 1:
            xs = [op(xs[i], xs[i + 1]) if i + 1 < len(xs) else xs[i] for i in range(0, len(xs), 2)]
        return xs[0]

    def attend(blocks, mask):
        s_blocks = []
        for k_ref, _, bias in blocks:
            n_col = k_ref.shape[0] // LANES
            r = _dot_nt(q_sc[...], k_ref[...].astype(BF16))
            s_cols = []
            for c in range(n_col):
                e = r[0:n_new, c * LANES:(c + 1) * LANES]
                for h in range(1, H_C):
                    e = jnp.where(is_head[h], r[h * n_new:(h + 1) * n_new, c * LANES:(c + 1) * LANES], e)
                e = e + bias[c:c + 1, :]
                if mask is not None:
                    e = jnp.where(mask, e, NEG)
                s_cols.append(e)
            s_blocks.append(s_cols)
        m_prev = m_sc[...]
        m_all = tree(jnp.maximum, [e for s_cols in s_blocks for e in s_cols])
        m_new = jnp.maximum(m_prev, _head_allreduce(m_all, jnp.maximum))
        alpha = jnp.exp(m_prev - m_new)
        p_blocks = [[jnp.exp(e - m_new) for e in s_cols] for s_cols in s_blocks]
        p_sum = tree(jnp.add, [p for p_cols in p_blocks for p in p_cols])
        m_sc[...] = m_new
        l_sc[...] = l_sc[...] * alpha + _head_allreduce(p_sum, jnp.add)
        pvs = []
        for (_, v_ref, _), p_cols in zip(blocks, p_blocks):
            p_rows = [jnp.concatenate([jnp.where(is_head[h], p, 0.0) for p in p_cols], axis=1)
                      for h in range(H_C)]
            pvs.append(_dot(jnp.concatenate(p_rows, axis=0).astype(BF16), v_ref[...].astype(BF16)))
        pv = tree(jnp.add, pvs)
        for h in range(H_C):
            a_h = jnp.broadcast_to(alpha[:, h:h + 1], (n_new, HD_C))
            acc_sc[h] = acc_sc[h] * a_h + pv[h * n_new:(h + 1) * n_new, :]

    @pl.when(step == 0)
    def _():
        q_sc[...] = (q_ref[...] * (HD_C ** -0.5)).astype(BF16)
        m_sc[...] = jnp.full_like(m_sc, NEG)
        l_sc[...] = jnp.zeros_like(l_sc)
        acc_sc[...] = jnp.zeros_like(acc_sc)
        carry_sc[...] = jnp.zeros_like(carry_sc)
        q_i = lax.broadcasted_iota(jnp.int32, (n_new, LANES), 0)
        attend([(kn_ref, vn_ref, -cn_ref[...])], lane // H_C <= q_i)

    n = pp * (PAGE // FOX_POS_PER_VREG)
    x = jnp.concatenate([lf_refs[u][...] for u in range(pp)], axis=0)
    a_i = lax.broadcasted_iota(jnp.int32, (LANES, 2 * LANES), 0)
    b_i = lax.broadcasted_iota(jnp.int32, (LANES, 2 * LANES), 1)
    same_head = a_i % H_C == b_i % H_C
    hw = (same_head & ((b_i < LANES) | (a_i // H_C > (b_i - LANES) // H_C))).astype(BF16)
    z = _dot(jnp.concatenate(_split3(x), axis=0), hw)
    z = z[0:n] + z[n:2 * n] + z[2 * n:3 * n]
    y, within = z[:, :LANES], z[:, LANES:]
    r_i = lax.broadcasted_iota(jnp.int32, (n, n), 0)
    c_i = lax.broadcasted_iota(jnp.int32, (n, n), 1)
    groups = PAGE // FOX_POS_PER_VREG
    later = ((c_i // groups < r_i // groups) | ((c_i // groups == r_i // groups) & (c_i > r_i))).astype(BF16)
    t = _dot(later, jnp.concatenate(_split3(y), axis=1))
    bias_all = t[:, :LANES] + t[:, LANES:2 * LANES] + t[:, 2 * LANES:] + within + carry_sc[...]
    carry_sc[...] = carry_sc[...] + jnp.sum(y, axis=0, keepdims=True)
    attend([(k_refs[u], v_refs[u], bias_all[u * groups:(u + 1) * groups, :]) for u in range(pp)], None)

    @pl.when(step == pl.num_programs(1) - 1)
    def _():
        inv_l = 1.0 / l_sc[...]
        o = jnp.concatenate([acc_sc[h] * jnp.broadcast_to(inv_l[:, h:h + 1], (n_new, HD_C))
                             for h in range(H_C)], axis=1)
        o_ref[...] = o * _silu(g_ref[...])


def fox_decode(q_rows, k_new, v_new, gate, cnew, cache_k, cache_v, lf_pool, page_table):
    n_batch, n_new, _ = gate.shape
    assert n_new * H_C == LANES
    n_pages = page_table.shape[1]
    pp = FOX_PAGES_PER_STEP
    n_steps = n_pages // pp
    groups = PAGE // FOX_POS_PER_VREG

    def page(u):
        return lambda b, i, pt: (pt[b * n_pages + n_pages - 1 - (i * pp + u)], 0, 0)

    per_batch = lambda b, i, pt: (b, 0, 0)
    in_specs = [pl.BlockSpec((None, LANES, HD_C), per_batch),
                pl.BlockSpec((None, LANES, HD_C), per_batch),
                pl.BlockSpec((None, LANES, HD_C), per_batch),
                pl.BlockSpec((None, n_new, W_C), per_batch),
                pl.BlockSpec((None, 1, LANES), per_batch)]
    in_specs += [pl.BlockSpec((None, PAGE * H_C, HD_C), page(u)) for u in range(pp)]
    in_specs += [pl.BlockSpec((None, PAGE * H_C, HD_C), page(u)) for u in range(pp)]
    in_specs += [pl.BlockSpec((None, groups, LANES), page(u)) for u in range(pp)]
    grid_spec = pltpu.PrefetchScalarGridSpec(
        num_scalar_prefetch=1,
        grid=(n_batch, n_steps),
        in_specs=in_specs,
        out_specs=pl.BlockSpec((None, n_new, W_C), per_batch),
        scratch_shapes=[pltpu.VMEM((LANES, HD_C), BF16),
                        pltpu.VMEM((n_new, LANES), F32), pltpu.VMEM((n_new, LANES), F32),
                        pltpu.VMEM((H_C, n_new, HD_C), F32), pltpu.VMEM((1, LANES), F32)])
    return pl.pallas_call(
        functools.partial(_fox_decode_kernel, n_new=n_new),
        out_shape=jax.ShapeDtypeStruct((n_batch, n_new, W_C), F32),
        grid_spec=grid_spec,
        compiler_params=_cparams(("parallel", "arbitrary")),
        name="fox_decode",
    )(page_table.reshape(-1), q_rows, k_new, v_new, gate, cnew,
      *([cache_k] * pp), *([cache_v] * pp), *([lf_pool] * pp))


RWKV_GROUP = 256
RWKV_BATCH_PER_STEP = 2
RWKV_STATE_BLOCK = (H_A // 2, LANES, LANES)


def _seg_sum(x, gmat):
    n = x.shape[0]
    n_grp = W_A // RWKV_GROUP
    parts = [x[:, c * RWKV_GROUP:(c + 1) * RWKV_GROUP] for c in range(n_grp)]
    pieces = _split3(jnp.concatenate(parts, axis=0))
    r = _dot(jnp.concatenate(pieces, axis=0), gmat)
    m = n_grp * n
    r = r[0:m] + r[m:2 * m] + r[2 * m:3 * m]
    return jnp.concatenate([r[c * n:(c + 1) * n] for c in range(n_grp)], axis=1)


def _cumsum_rows(tri, x):
    w = x.shape[1]
    r = _dot(tri, jnp.concatenate(_split3(x), axis=1))
    return r[:, 0:w] + r[:, w:2 * w] + r[:, 2 * w:3 * w]


def _rwkv_kernel(z_ref, sp_ref, s0_ref, *refs):
    o_ref, so_ref, state_sc, prev_sc = refs[-4:]

    @pl.when(pl.program_id(1) == 0)
    def _():
        state_sc[...] = s0_ref[...]
        prev_sc[...] = sp_ref[...]

    res = [_rwkv_one(z_ref.at[u], *refs[:-4], state_sc.at[u], prev_sc.at[u]) for u in range(z_ref.shape[0])]
    o_ref[...] = jnp.stack([r[0] for r in res]).astype(o_ref.dtype)
    states = jnp.stack([r[1] for r in res])
    state_sc[...] = states
    so_ref[...] = states
    prev_sc[...] = jnp.stack([r[2] for r in res])


def _rwkv_one(z_ref, mu_ref, w0_ref, a0_ref, lora_ref, kk_ref, ka_ref, rk_ref,
              lng_ref, lnb_ref, state_sc, prev_sc):
    L = CHUNK
    n_in = z_ref.shape[0]
    z = z_ref[...]
    if n_in < L:
        z = jnp.concatenate([z, jnp.zeros((L - n_in, z.shape[1]), F32)], axis=0)
    cols = z[:, :SHIFT_W]
    gate = z[:, SHIFT_W:SHIFT_W + W_A]
    row = lax.broadcasted_iota(jnp.int32, (L, 1), 0)
    prev = jnp.where(row == 0, prev_sc[...], pltpu.roll(cols, 1, axis=0))
    xs = cols + (prev - cols) * mu_ref[...]
    r = xs[:, 0:W_A]
    k = xs[:, W_A:2 * W_A]
    v = xs[:, 2 * W_A:3 * W_A]
    lo = xs[:, 3 * W_A:SHIFT_W]
    lane = lax.broadcasted_iota(jnp.int32, (L, LANES), 1)
    first = lane < N_A
    lo = jnp.where(lane < LORA, jnp.tanh(lo), lo)
    lo_hi = lo.astype(BF16)
    lo_lo = (lo - lo_hi.astype(F32)).astype(BF16)
    up = _dot(jnp.concatenate([lo_hi, lo_hi, lo_lo], axis=1), lora_ref[...])
    logw = -float(np.exp(-0.5)) * jax.nn.sigmoid(w0_ref[...] + up[:, :W_A])
    lr = jax.nn.sigmoid(a0_ref[...] + up[:, W_A:])

    g_i = lax.broadcasted_iota(jnp.int32, (RWKV_GROUP, RWKV_GROUP), 0)
    g_j = lax.broadcasted_iota(jnp.int32, (RWKV_GROUP, RWKV_GROUP), 1)
    gmat = (g_i // N_A == g_j // N_A).astype(BF16)

    kk = k * kk_ref[...]
    kk = kk / jnp.maximum(jnp.sqrt(_seg_sum(kk * kk, gmat)), 1e-12)
    k = k * (1.0 + (lr - 1.0) * ka_ref[...])
    if n_in < L:
        real = row < n_in
        logw = jnp.where(real, logw, 0.0)
        kk = jnp.where(real, kk, 0.0)
        k = jnp.where(real, k, 0.0)

    t_i = lax.broadcasted_iota(jnp.int32, (L, L), 0)
    s_i = lax.broadcasted_iota(jnp.int32, (L, L), 1)
    cum = _cumsum_rows((s_i <= t_i).astype(BF16), logw)
    cum_l = cum[L - 1:L, :]
    g_incl = jnp.exp(cum)
    g_inv = jnp.exp(-cum)
    g_tail = jnp.exp(cum_l - cum)
    abar = -kk * jnp.exp(cum - logw)
    b = kk * lr
    bbar = b * g_inv
    kbar = k * g_inv
    rbar = r * g_incl
    btail = b * g_tail
    ktail = k * g_tail
    g_l = jnp.exp(cum_l)

    n2 = 2 * L
    r_i = lax.broadcasted_iota(jnp.int32, (n2, n2), 0)
    c_i = lax.broadcasted_iota(jnp.int32, (n2, n2), 1)
    same = r_i // L == c_i // L
    strict = same & (c_i % L < r_i % L)
    incl = same & (c_i % L <= r_i % L)
    eye = (r_i == c_i).astype(F32)

    def stack(x, sl):
        xp = x[:, sl]
        return jnp.concatenate([jnp.where(first, xp, 0.0), jnp.where(first, 0.0, xp)], axis=0)

    sls = [slice(p * LANES, (p + 1) * LANES) for p in range(H_A // 2)]
    bf = lambda xs_: [x.astype(BF16) for x in xs_]
    xar = bf([jnp.concatenate([stack(abar, sl), stack(rbar, sl)], axis=0) for sl in sls])
    xbk = bf([jnp.concatenate([stack(bbar, sl), stack(kbar, sl)], axis=0) for sl in sls])
    vst = bf([stack(v, sl) for sl in sls])
    tails = [jnp.concatenate([stack(btail, sl), stack(ktail, sl)], axis=0).astype(BF16) for sl in sls]
    big = [_dot_nt(l_, r_) for l_, r_ in zip(xar, xbk)]
    a_ab = [jnp.where(strict, g[:n2, :n2], 0.0) for g in big]
    a_rb = bf([jnp.where(incl, g[n2:, :n2], 0.0) for g in big])
    a_kk = bf([jnp.concatenate([jnp.where(strict, g[:n2, n2:], 0.0), jnp.where(incl, g[n2:, n2:], 0.0)], axis=0)
               for g in big])
    tinv = [eye + a for a in a_ab]
    pw = [_dot(x, x) for x in bf(a_ab)]
    n_steps = int(np.log2(L)) - 1
    for step in range(n_steps):
        pw_bf = bf(pw)
        if step == n_steps - 1:
            tinv = [t + _dot(t.astype(BF16), x) for t, x in zip(tinv, pw_bf)]
        else:
            both = [_dot(jnp.concatenate([x, t.astype(BF16)], axis=0), x) for x, t in zip(pw_bf, tinv)]
            pw = [m_[:n2] for m_ in both]
            tinv = [t + m_[n2:] for t, m_ in zip(tinv, both)]
    s_old = [state_sc[p] for p in range(H_A // 2)]
    ars = [_dot_nt(l_, s_) for l_, s_ in zip(xar, bf(s_old))]
    akv = [_dot(a_, v_) for a_, v_ in zip(a_kk, vst)]
    u_bf = bf([_dot(t.astype(BF16), (x[:n2] + w_[:n2]).astype(BF16)) for t, x, w_ in zip(tinv, ars, akv)])
    yst = [x[n2:] + w_[n2:] + _dot(rb_, u_) for x, w_, rb_, u_ in zip(ars, akv, a_rb, u_bf)]
    y = jnp.concatenate([t[:L] + t[L:] for t in yst], axis=1)
    upd = [_dot_tn(jnp.concatenate([u_, v_], axis=0), tl) for u_, v_, tl in zip(u_bf, vst, tails)]
    new_states = [s * g_l[:, sl] + d for s, sl, d in zip(s_old, sls, upd)]

    mean = _seg_sum(y, gmat) * (1.0 / N_A)
    dev = y - mean
    var = _seg_sum(dev * dev, gmat) * (1.0 / N_A)
    yn = dev * lax.rsqrt(var + GN_EPS) * lng_ref[...] + lnb_ref[...]
    bonus = _seg_sum(r * k * rk_ref[...], gmat) * v
    out = (yn + bonus) * _silu(gate)
    return out[:n_in], jnp.stack(new_states), cols[n_in - 1:n_in, :]


def rwkv7(z, shift_prev, s0_blk, mu, w0, w_up, a0, a_up, k_k, k_a, r_k, ln_g, ln_b, n_batch, t, out_dtype):
    n_in = min(t, CHUNK)
    nc = t // n_in
    bpg = RWKV_BATCH_PER_STEP
    assert n_batch % bpg == 0
    lora = jnp.zeros((2 * LORA, 2 * W_A), F32).at[:LORA, :W_A].set(w_up).at[LORA:, W_A:].set(a_up)
    lora_hi = lora.astype(BF16)
    lora_lo = (lora - lora_hi.astype(F32)).astype(BF16)
    lora3 = jnp.concatenate([lora_hi, lora_lo, lora_hi], axis=0)
    vec = lambda a, n: a.reshape(1, n)
    const = lambda shape: pl.BlockSpec(shape, lambda b, c: (0,) * len(shape))
    state_spec = pl.BlockSpec((bpg,) + RWKV_STATE_BLOCK, lambda b, c: (b, 0, 0, 0))
    out, s_blk = pl.pallas_call(
        _rwkv_kernel,
        out_shape=(jax.ShapeDtypeStruct((n_batch, t, W_A), out_dtype),
                   jax.ShapeDtypeStruct(s0_blk.shape, F32)),
        grid=(n_batch // bpg, nc),
        in_specs=[pl.BlockSpec((bpg, n_in, SHIFT_W + W_A), lambda b, c: (b, c, 0)),
                  pl.BlockSpec((bpg, 1, SHIFT_W), lambda b, c: (b, 0, 0)),
                  state_spec,
                  const((1, SHIFT_W)), const((1, W_A)), const((1, W_A)), const((6 * LORA, 2 * W_A)),
                  const((1, W_A)), const((1, W_A)), const((1, W_A)), const((1, W_A)), const((1, W_A))],
        out_specs=(pl.BlockSpec((bpg, n_in, W_A), lambda b, c: (b, c, 0)), state_spec),
        scratch_shapes=[pltpu.VMEM((bpg,) + RWKV_STATE_BLOCK, F32), pltpu.VMEM((bpg, 1, SHIFT_W), F32)],
        compiler_params=_cparams(("parallel", "arbitrary")),
        name="rwkv7",
    )(z.reshape(n_batch, t, SHIFT_W + W_A), shift_prev.reshape(n_batch, 1, SHIFT_W), s0_blk,
      vec(mu, SHIFT_W), vec(w0, W_A), vec(a0, W_A), lora3,
      vec(k_k, W_A), vec(k_a, W_A), vec(r_k, W_A), vec(ln_g, W_A), vec(ln_b, W_A))
    return out.reshape(n_batch * t, W_A), s_blk


def _state_blockdiag(s):
    b = s.shape[0]
    hq = RWKV_STATE_BLOCK[1] // N_A
    s = s.reshape(b, H_A // hq, hq, N_A, N_A)
    blk = jnp.einsum("bgaij,ac->bgaicj", s, jnp.eye(hq, dtype=s.dtype))
    return blk.reshape((b,) + RWKV_STATE_BLOCK)


def _state_diag_blocks(sb):
    b = sb.shape[0]
    hq = RWKV_STATE_BLOCK[1] // N_A
    blk = sb.reshape(b, H_A // hq, hq, N_A, hq, N_A)
    return jnp.stack([blk[:, :, a, :, a, :] for a in range(hq)], axis=2).reshape(b, H_A, N_A, N_A)


_E_SIZES = (SHIFT_W, W_A, W_B, KV_B * HD_B, KV_B * HD_B, W_B, W_M, W_M)
_E_OFF = np.concatenate([[0], np.cumsum(_E_SIZES)])
_O_SIZES = (W_C, W_C, W_C, H_C, W_C, W_M, W_M)
_O_OFF = np.concatenate([[0], np.cumsum(_O_SIZES)])
_SWA_K_BLOCK = 2 * W_B // LANES
_SWA_V_BLOCK = _SWA_K_BLOCK + 1
_FM_F_BLOCK = 2 * W_M // LANES


def _ecol(w, i, j=None):
    return w[:, int(_E_OFF[i]):int(_E_OFF[(i if j is None else j) + 1])]


def _ocol(w, i, j=None):
    return w[:, int(_O_OFF[i]):int(_O_OFF[(i if j is None else j) + 1])]


def _even_layer(x2d, n_batch, t, mk, mv, shift_prev, s0_blk, swa_cache, w, out_dtype):
    xn = rmsnorm_cast(x2d, w["norm_e"])
    z_r = matmul(xn, w["w_rwkv"])
    z_s = matmul(xn, w["w_swa"])
    z_m = matmul(xn, w["w_mem0"])
    a_out, s_blk = rwkv7(z_r, shift_prev, s0_blk, *w["rwkv"], n_batch, t, out_dtype)
    new_shift = z_r.reshape(n_batch, t, -1)[:, -1, :SHIFT_W]
    k_new = z_s[:, 2 * W_B:2 * W_B + LANES].reshape(n_batch, t, LANES)
    v_new = z_s[:, 2 * W_B + LANES:].reshape(n_batch, t, LANES)
    if swa_cache is None:
        nq = t // WINDOW
        cur = lambda c: (lambda b, i: (b * nq + i, c))
        prev = lambda c: (lambda b, i: (b * nq + jnp.maximum(i - 1, 0), c))
        b_out = swa_attn(w["sink"], z_s, 0, 1, (z_s, prev(_SWA_K_BLOCK)), (z_s, cur(_SWA_K_BLOCK)),
                         (z_s, prev(_SWA_V_BLOCK)), (z_s, cur(_SWA_V_BLOCK)), n_batch, t, True, out_dtype)
        k_win, v_win = k_new[:, -WINDOW:], v_new[:, -WINDOW:]
    else:
        k_buf, v_buf = (c.reshape(n_batch, WINDOW, LANES) for c in swa_cache)
        pad = lambda a: jnp.pad(a, ((0, 0), (0, WINDOW - t), (0, 0)))
        per_b = lambda b, i: (b, 0, 0)
        b_out = swa_attn(w["sink"], z_s, 0, 1, (k_buf, per_b), (pad(k_new), per_b),
                         (v_buf, per_b), (pad(v_new), per_b), n_batch, t, False, out_dtype)
        k_win = jnp.concatenate([k_buf, k_new], axis=1)[:, -WINDOW:]
        v_win = jnp.concatenate([v_buf, v_new], axis=1)[:, -WINDOW:]
    m_out = mem_attn(z_m, 0, 1, mk, mv, n_batch, t, out_dtype)
    x1, xn1 = outproj([a_out, b_out, m_out], w["w_out_e"], x2d, w["norm_o"], False)
    kv_shape = (n_batch, WINDOW, KV_B, HD_B)
    return x1, xn1, (new_shift, _state_diag_blocks(s_blk), k_win.reshape(kv_shape), v_win.reshape(kv_shape))


def _odd_layer(x1, xn1, n_batch, t, mk, mv, fox_past, w, out_dtype):
    q = matmul(xn1, w["w_q"])
    k = matmul(xn1, w["w_k"])
    v = matmul(xn1, w["w_v"])
    gate = matmul(xn1, w["w_gc"])
    z_fm = matmul(xn1, w["w_fm"])
    if fox_past is None:
        logf, ct = fox_gates(z_fm, _FM_F_BLOCK, w["fox_bf"], n_batch, t, t)
        c_out = fox_prompt(q, k, v, gate, ct, n_batch, t)
    else:
        cache_k, cache_v, lf_pool, page_table = fox_past
        logf, cs = fox_gates(z_fm, _FM_F_BLOCK, w["fox_bf"], 1, n_batch * t, t)
        q_rows = jnp.swapaxes(q.reshape(n_batch, t, H_C, HD_C), 1, 2).reshape(n_batch, H_C * t, HD_C)
        pos_head_rows = lambda a: a.reshape(n_batch, t * H_C, HD_C)
        c_out = fox_decode(q_rows, pos_head_rows(k), pos_head_rows(v), gate.reshape(n_batch, t, W_C),
                           cs.reshape(n_batch, 1, t * H_C), cache_k, cache_v, lf_pool, page_table)
        c_out = c_out.reshape(n_batch * t, W_C)
    m_out = mem_attn(z_fm, 0, 1, mk, mv, n_batch, t, out_dtype)
    y = outproj([c_out, m_out], w["w_out_o"], x1, w["norm_f"], True)
    kv_shape = (n_batch, t, H_C, HD_C)
    return y, (k.reshape(kv_shape), v.reshape(kv_shape), logf.reshape(n_batch, t, H_C))


def kernel(x_prompt, x_sample, state_rwkv, state_shift, cache_swa_k, cache_swa_v, cache_mem_k, cache_mem_v, cache_fox_k, cache_fox_v, cache_fox_logf, page_table, mem_prompt, norm_e, w_in_e, rwkv_mu, rwkv_w0, rwkv_w_up, rwkv_a0, rwkv_a_up, rwkv_kk, rwkv_ka, rwkv_rk, rwkv_ln_g, rwkv_ln_b, swa_sink, w_out_e, norm_o, w_in_o, fox_bf, w_out_o, mem_norm, w_mem_k, w_mem_v, norm_f):
    n_b, t, d = x_prompt.shape
    n_db, s, _ = x_sample.shape
    assert state_rwkv.shape[0] == 1 and cache_fox_k.shape[0] == 1 and mem_norm.shape[0] == 2
    bf = lambda a: a.astype(BF16)
    we, wo = w_in_e[0], w_in_o[0]
    woe, woo = bf(w_out_e[0]), bf(w_out_o[0])
    w = {
        "norm_e": norm_e[0], "norm_o": norm_o[0], "norm_f": norm_f, "sink": swa_sink[0], "fox_bf": fox_bf[0],
        "w_rwkv": bf(_ecol(we, 0, 1)),
        "w_swa": bf(jnp.concatenate([_ecol(we, 2), _ecol(we, 5), _ecol(we, 3, 4)], axis=1)),
        "w_mem0": bf(_ecol(we, 6, 7)),
        "rwkv": (rwkv_mu[0], rwkv_w0[0], rwkv_w_up[0], rwkv_a0[0], rwkv_a_up[0], rwkv_kk[0], rwkv_ka[0],
                 rwkv_rk[0].reshape(-1), rwkv_ln_g[0], rwkv_ln_b[0]),
        "w_out_e": [woe[:W_A], woe[W_A:W_A + W_B], woe[W_A + W_B:]],
        "w_q": bf(_ocol(wo, 0)), "w_k": bf(_ocol(wo, 1)), "w_v": bf(_ocol(wo, 2)), "w_gc": bf(_ocol(wo, 4)),
        "w_fm": bf(jnp.concatenate([_ocol(wo, 5, 6), _ocol(wo, 3), jnp.zeros((d, LANES - H_C), F32)], axis=1)),
        "w_out_o": [woo[:W_C], woo[W_C:]],
    }

    p_mem = []
    for layer in range(2):
        hm = rmsnorm_cast(mem_prompt.reshape(n_b * N_MEM, d), mem_norm[layer])
        kv = matmul(hm, bf(jnp.concatenate([w_mem_k[layer], w_mem_v[layer]], axis=1)))
        p_mem.append((kv[:, :W_M].reshape(n_b, N_MEM, W_M), kv[:, W_M:].reshape(n_b, N_MEM, W_M)))
    xp = x_prompt.reshape(n_b * t, d)
    xp1, xpn1, (p_shift, p_rwkv, p_swa_k, p_swa_v) = _even_layer(
        xp, n_b, t, p_mem[0][0], p_mem[0][1], jnp.zeros((n_b, SHIFT_W), F32),
        jnp.zeros((n_b,) + RWKV_STATE_BLOCK, F32), None, w, BF16)
    y_prompt, (p_fox_k, p_fox_v, p_fox_lf) = _odd_layer(xp1, xpn1, n_b, t, p_mem[1][0], p_mem[1][1], None, w, BF16)

    xs = x_sample.reshape(n_db * s, d)
    memc = lambda c, layer: c[layer].reshape(n_db, N_MEM, W_M)
    xs1, xsn1, (s_shift, s_rwkv, s_swa_k, s_swa_v) = _even_layer(
        xs, n_db, s, memc(cache_mem_k, 0), memc(cache_mem_v, 0), state_shift[0], _state_blockdiag(state_rwkv[0]),
        (cache_swa_k[0], cache_swa_v[0]), w, F32)
    n_pool = cache_fox_k.shape[1]
    fox_past = (cache_fox_k.reshape(n_pool, PAGE * H_C, HD_C), cache_fox_v.reshape(n_pool, PAGE * H_C, HD_C),
                cache_fox_logf.reshape(n_pool, PAGE // FOX_POS_PER_VREG, LANES), page_table)
    y_sample, (s_fox_k, s_fox_v, s_fox_lf) = _odd_layer(xs1, xsn1, n_db, s, memc(cache_mem_k, 1), memc(cache_mem_v, 1),
                                                       fox_past, w, F32)

    mem_shape = (n_b, N_MEM, H_M, HD_M)
    return (y_prompt.reshape(n_b, t, d), y_sample.reshape(n_db, s, d),
            p_rwkv[None], p_shift[None], p_swa_k[None], p_swa_v[None],
            jnp.stack([p_mem[0][0].reshape(mem_shape), p_mem[1][0].reshape(mem_shape)]),
            jnp.stack([p_mem[0][1].reshape(mem_shape), p_mem[1][1].reshape(mem_shape)]),
            p_fox_k[None], p_fox_v[None], p_fox_lf[None],
            s_rwkv[None], s_shift[None], s_swa_k[None], s_swa_v[None],
            s_fox_k[None], s_fox_v[None], s_fox_lf[None])
```

```python
import functools

import jax
import jax.numpy as jnp
import numpy as np
from jax import lax
from jax.experimental import pallas as pl
from jax.experimental.pallas import tpu as pltpu

F32 = jnp.float32
BF16 = jnp.bfloat16
HIGHEST = lax.Precision.HIGHEST

D_MODEL = 2048
EPS = 1e-6
NEG = -1e30
H_A, N_A = 16, 64
W_A = H_A * N_A
LORA = 64
SHIFT_W = 3 * W_A + 2 * LORA
GN_EPS = 64e-5
CHUNK = 64
H_B, KV_B, HD_B = 16, 2, 64
W_B = H_B * HD_B
WINDOW = 128
H_C, HD_C = 16, 128
W_C = H_C * HD_C
PAGE = 128
N_MEM, H_M, HD_M = 256, 4, 64
W_M = H_M * HD_M

LANES = 128
VMEM_LIMIT = 48 * 1024 * 1024


def _cparams(sem):
    return pltpu.CompilerParams(dimension_semantics=sem, vmem_limit_bytes=VMEM_LIMIT)


def _dot(a, b):
    return jnp.dot(a, b, preferred_element_type=F32)


def _dot_nt(a, b):
    return lax.dot_general(a, b, (((1,), (1,)), ((), ())), preferred_element_type=F32)


def _dot_tn(a, b):
    return lax.dot_general(a, b, (((0,), (0,)), ((), ())), preferred_element_type=F32)


def _dot_hi(a, b):
    return jnp.dot(a, b, preferred_element_type=F32, precision=HIGHEST)


def _silu(g):
    return g * jax.nn.sigmoid(g)


def _largest_tile(n, cap, mult):
    best = None
    for t in range(mult, min(n, cap) + 1, mult):
        if n % t == 0:
            best = t
    assert best is not None, (n, cap, mult)
    return best


def _rmsnorm_kernel(x_ref, g_ref, o_ref):
    x = x_ref[...]
    ms = jnp.mean(x * x, axis=-1, keepdims=True)
    o_ref[...] = (x * lax.rsqrt(ms + EPS) * g_ref[...]).astype(o_ref.dtype)


def rmsnorm_cast(x, g):
    m, d = x.shape
    tm = _largest_tile(m, 512, 8)
    return pl.pallas_call(
        _rmsnorm_kernel,
        out_shape=jax.ShapeDtypeStruct((m, d), BF16),
        grid=(m // tm,),
        in_specs=[pl.BlockSpec((tm, d), lambda i: (i, 0)),
                  pl.BlockSpec((1, d), lambda i: (0, 0))],
        out_specs=pl.BlockSpec((tm, d), lambda i: (i, 0)),
        compiler_params=_cparams(("parallel",)),
        name="rmsnorm_cast",
    )(x, g.reshape(1, d))


def _matmul_kernel(x_ref, w_ref, o_ref):
    o_ref[...] = _dot(x_ref[...], w_ref[...])


def matmul(x, w):
    m, k = x.shape
    n = w.shape[1]
    tm = _largest_tile(m, 1024, 8)
    tn = _largest_tile(n, 1536, LANES)
    return pl.pallas_call(
        _matmul_kernel,
        out_shape=jax.ShapeDtypeStruct((m, n), F32),
        grid=(m // tm, n // tn),
        in_specs=[pl.BlockSpec((tm, k), lambda i, j: (i, 0)),
                  pl.BlockSpec((k, tn), lambda i, j: (0, j))],
        out_specs=pl.BlockSpec((tm, tn), lambda i, j: (i, j)),
        compiler_params=_cparams(("parallel", "parallel")),
        name="matmul",
    )(x, w)


def _outproj_kernel(*refs, n_in, final):
    hs = refs[:n_in]
    ws = refs[n_in:2 * n_in]
    x_ref, g_ref = refs[2 * n_in], refs[2 * n_in + 1]
    outs = refs[2 * n_in + 2:]
    acc = x_ref[...]
    for h_ref, w_ref in zip(hs, ws):
        acc = acc + _dot(h_ref[...].astype(BF16), w_ref[...])
    ms = jnp.mean(acc * acc, axis=-1, keepdims=True)
    normed = acc * lax.rsqrt(ms + EPS) * g_ref[...]
    if final:
        outs[0][...] = normed
    else:
        outs[0][...] = acc
        outs[1][...] = normed.astype(BF16)


def outproj(hs, ws, x, g, final):
    m, d = x.shape
    tm = _largest_tile(m, 256, 16)
    n_in = len(hs)
    in_specs = [pl.BlockSpec((tm, h.shape[1]), lambda i: (i, 0)) for h in hs]
    in_specs += [pl.BlockSpec(w.shape, lambda i: (0, 0)) for w in ws]
    in_specs += [pl.BlockSpec((tm, d), lambda i: (i, 0)), pl.BlockSpec((1, d), lambda i: (0, 0))]
    row = pl.BlockSpec((tm, d), lambda i: (i, 0))
    if final:
        out_shape, out_specs = jax.ShapeDtypeStruct((m, d), F32), row
    else:
        out_shape = (jax.ShapeDtypeStruct((m, d), F32), jax.ShapeDtypeStruct((m, d), BF16))
        out_specs = (row, row)
    return pl.pallas_call(
        functools.partial(_outproj_kernel, n_in=n_in, final=final),
        out_shape=out_shape,
        grid=(m // tm,),
        in_specs=in_specs,
        out_specs=out_specs,
        compiler_params=_cparams(("parallel",)),
        name="outproj",
    )(*hs, *ws, x, g.reshape(1, d))


def _mem_attn_kernel(q_ref, g_ref, k_ref, v_ref, o_ref):
    tq = q_ref.shape[0]
    lane = lax.broadcasted_iota(jnp.int32, (tq, LANES), 1)
    first = lane < HD_M
    heads = range(H_M)
    sls = [slice((h // 2) * LANES, (h // 2 + 1) * LANES) for h in heads]
    sel = [first if h % 2 == 0 else jnp.logical_not(first) for h in heads]
    qs = [jnp.where(sel[h], q_ref[:, sls[h]] * (HD_M ** -0.5), 0.0).astype(BF16) for h in heads]
    kp = [k_ref[:, sls[h]].astype(BF16) for h in heads]
    vp = [v_ref[:, sls[h]].astype(BF16) for h in heads]
    s = [_dot_nt(qs[h], kp[h]) for h in heads]
    m = [jnp.max(s[h], axis=-1, keepdims=True) for h in heads]
    e = [jnp.exp(s[h] - m[h]) for h in heads]
    pr = [e[h] / jnp.sum(e[h], axis=-1, keepdims=True) for h in heads]
    o = [_dot(pr[h].astype(BF16), vp[h]) for h in heads]
    outs = [jnp.where(first, o[2 * p], o[2 * p + 1]) * _silu(g_ref[:, p * LANES:(p + 1) * LANES])
            for p in range(H_M // 2)]
    o_ref[...] = jnp.concatenate(outs, axis=-1).astype(o_ref.dtype)


def mem_attn(z, q_col, g_col, mk, mv, n_batch, t, out_dtype):
    tq = _largest_tile(t, 512, 8)
    nq = t // tq
    return pl.pallas_call(
        _mem_attn_kernel,
        out_shape=jax.ShapeDtypeStruct((n_batch * t, W_M), out_dtype),
        grid=(n_batch, nq),
        in_specs=[pl.BlockSpec((tq, W_M), lambda b, i: (b * nq + i, q_col)),
                  pl.BlockSpec((tq, W_M), lambda b, i: (b * nq + i, g_col)),
                  pl.BlockSpec((None, N_MEM, W_M), lambda b, i: (b, 0, 0)),
                  pl.BlockSpec((None, N_MEM, W_M), lambda b, i: (b, 0, 0))],
        out_specs=pl.BlockSpec((tq, W_M), lambda b, i: (b * nq + i, 0)),
        compiler_params=_cparams(("parallel", "parallel")),
        name="mem_attn",
    )(z, z, mk, mv)


def _alibi_slope(h):
    return float(2.0 ** (-8.0 * (h + 1) / H_B))


def _swa_kernel(sink_ref, q_ref, g_ref, kp_ref, kc_ref, vp_ref, vc_ref, o_ref, *, prompt):
    tq = q_ref.shape[0]
    nk = 2 * WINDOW
    lane = lax.broadcasted_iota(jnp.int32, (tq, LANES), 1)
    first = lane < HD_B
    klane = lax.broadcasted_iota(jnp.int32, (nk, LANES), 1)
    qi = lax.broadcasted_iota(jnp.int32, (tq, nk), 0)
    kj = lax.broadcasted_iota(jnp.int32, (tq, nk), 1)
    dist = qi + WINDOW - kj
    valid = (dist >= 0) & (dist < WINDOW)
    if prompt:
        has_prev = pl.program_id(1) > 0
        valid = valid & (has_prev | (kj >= WINDOW))
    distf = dist.astype(F32)
    kband = jnp.concatenate([kp_ref[...], kc_ref[...]], axis=0)
    vband = jnp.concatenate([vp_ref[...], vc_ref[...]], axis=0)
    kroll = pltpu.roll(kband, HD_B, axis=1)
    vroll = pltpu.roll(vband, HD_B, axis=1)
    kfirst = klane < HD_B
    kdup = [jnp.where(kfirst, kband, kroll).astype(BF16), jnp.where(kfirst, kroll, kband).astype(BF16)]
    vdup = [jnp.where(kfirst, vband, vroll).astype(BF16), jnp.where(kfirst, vroll, vband).astype(BF16)]
    group = H_B // KV_B
    heads = range(H_B)
    qs = [q_ref[:, (h // 2) * LANES:(h // 2 + 1) * LANES] * (HD_B ** -0.5) for h in heads]
    qs = [jnp.where(first if h % 2 == 0 else jnp.logical_not(first), qs[h], 0.0).astype(BF16) for h in heads]
    s = [_dot_nt(qs[h], kdup[h // group]) for h in heads]
    s = [jnp.where(valid, s[h] - _alibi_slope(h) * distf, NEG) for h in heads]
    sink = [sink_ref[h] for h in heads]
    m = [jnp.maximum(jnp.max(s[h], axis=-1, keepdims=True), sink[h]) for h in heads]
    e = [jnp.exp(s[h] - m[h]) for h in heads]
    pr = [e[h] / (jnp.sum(e[h], axis=-1, keepdims=True) + jnp.exp(sink[h] - m[h])) for h in heads]
    o = [_dot(pr[h].astype(BF16), vdup[h // group]) for h in heads]
    outs = [jnp.where(first, o[2 * p], o[2 * p + 1]) * _silu(g_ref[:, p * LANES:(p + 1) * LANES])
            for p in range(H_B // 2)]
    o_ref[...] = jnp.concatenate(outs, axis=-1).astype(o_ref.dtype)


def swa_attn(sink, zq, q_col, g_col, k_prev, k_cur, v_prev, v_cur, n_batch, t, prompt, out_dtype):
    tq = WINDOW if prompt else t
    nq = t // tq
    kv_specs = [pl.BlockSpec((WINDOW, LANES) if a.ndim == 2 else (None, WINDOW, LANES), im)
                for a, im in (k_prev, k_cur, v_prev, v_cur)]
    return pl.pallas_call(
        functools.partial(_swa_kernel, prompt=prompt),
        out_shape=jax.ShapeDtypeStruct((n_batch * t, W_B), out_dtype),
        grid=(n_batch, nq),
        in_specs=[pl.BlockSpec(memory_space=pltpu.SMEM),
                  pl.BlockSpec((tq, W_B), lambda b, i: (b * nq + i, q_col)),
                  pl.BlockSpec((tq, W_B), lambda b, i: (b * nq + i, g_col))] + kv_specs,
        out_specs=pl.BlockSpec((tq, W_B), lambda b, i: (b * nq + i, 0)),
        compiler_params=_cparams(("parallel", "arbitrary")),
        name="swa_attn",
    )(sink, zq, zq, k_prev[0], k_cur[0], v_prev[0], v_cur[0])


def _fox_gates_kernel(f_ref, bf_ref, lf_ref, ct_ref, carry_sc, *, seg, carry):
    tb = f_ref.shape[0]
    x = f_ref[...] + bf_ref[...]
    lf = jnp.minimum(x, 0.0) - jnp.log1p(jnp.exp(-jnp.abs(x)))
    lf_ref[...] = lf[:, :H_C]
    s_i = lax.broadcasted_iota(jnp.int32, (tb, tb), 0)
    t_i = lax.broadcasted_iota(jnp.int32, (tb, tb), 1)
    if carry:
        cum = _dot_hi(lf.T, (s_i <= t_i).astype(F32))

        @pl.when(pl.program_id(1) == 0)
        def _():
            carry_sc[...] = jnp.zeros_like(carry_sc)
        cum = cum + carry_sc[...]
        carry_sc[...] = cum[:, tb - 1:tb]
        ct_ref[...] = cum[:H_C, :]
    else:
        cum = _dot_hi(((t_i <= s_i) & (s_i // seg == t_i // seg)).astype(F32), lf)
        ct_ref[...] = cum[:, :H_C]


def fox_gates(z, f_col, bf, n_batch, t, seg):
    carry = seg == t
    tb = _largest_tile(t, 512, LANES) if carry else t
    assert carry or n_batch == 1
    nt = t // tb
    bf_pad = jnp.zeros((1, LANES), F32).at[0, :H_C].set(bf)
    if carry:
        c_shape = jax.ShapeDtypeStruct((n_batch, H_C, t), F32)
        c_spec = pl.BlockSpec((None, H_C, tb), lambda b, j: (b, 0, j))
    else:
        c_shape = jax.ShapeDtypeStruct((n_batch * t, H_C), F32)
        c_spec = pl.BlockSpec((tb, H_C), lambda b, j: (b * nt + j, 0))
    return pl.pallas_call(
        functools.partial(_fox_gates_kernel, seg=seg, carry=carry),
        out_shape=(jax.ShapeDtypeStruct((n_batch * t, H_C), F32), c_shape),
        grid=(n_batch, nt),
        in_specs=[pl.BlockSpec((tb, LANES), lambda b, j: (b * nt + j, f_col)),
                  pl.BlockSpec((1, LANES), lambda b, j: (0, 0))],
        out_specs=(pl.BlockSpec((tb, H_C), lambda b, j: (b * nt + j, 0)), c_spec),
        scratch_shapes=[pltpu.VMEM((LANES, 1), F32)],
        compiler_params=_cparams(("parallel", "arbitrary")),
        name="fox_gates",
    )(z, bf_pad)


FOX_TILE = 512
FOX_HEADS_PER_STEP = 2


def _fox_prompt_kernel(q_ref, k_ref, v_ref, g_ref, ct_ref, o_ref, kb_sc, vb_sc):
    tq = q_ref.shape[0]
    nh = FOX_HEADS_PER_STEP
    h0 = pl.program_id(1) * nh
    i = pl.program_id(2)

    @pl.when(i == 0)
    def _():
        kb_sc[...] = k_ref[...].astype(BF16)
        vb_sc[...] = v_ref[...].astype(BF16)

    lanes = [slice(u * HD_C, (u + 1) * HD_C) for u in range(nh)]
    q = [(q_ref[:, sl] * (HD_C ** -0.5)).astype(BF16) for sl in lanes]
    c_ref = [ct_ref[pl.ds(h0 + u, 1), pl.ds(i, 1), :][0][:, 0:1] for u in range(nh)]

    def chunk(j, carry, diagonal):
        off = pl.multiple_of(j * tq, tq)
        hs = range(nh)
        bias = [c_ref[u] - ct_ref[pl.ds(h0 + u, 1), pl.ds(j, 1), :][0] for u in hs]
        s = [_dot_nt(q[u], kb_sc[pl.ds(off, tq), lanes[u]]) + bias[u] for u in hs]
        if diagonal:
            r_i = lax.broadcasted_iota(jnp.int32, (tq, tq), 0)
            c_i = lax.broadcasted_iota(jnp.int32, (tq, tq), 1)
            s = [jnp.where(c_i <= r_i, x, NEG) for x in s]
        m_new = [jnp.maximum(carry[u][0], jnp.max(s[u], axis=-1, keepdims=True)) for u in hs]
        alpha = [jnp.exp(carry[u][0] - m_new[u]) for u in hs]
        p = [jnp.exp(s[u] - m_new[u]) for u in hs]
        l = [carry[u][1] * alpha[u] + jnp.sum(p[u], axis=-1, keepdims=True) for u in hs]
        pv = [_dot(p[u].astype(BF16), vb_sc[pl.ds(off, tq), lanes[u]]) for u in hs]
        return tuple((m_new[u], l[u], carry[u][2] * alpha[u] + pv[u]) for u in hs)

    init = tuple((jnp.full((tq, 1), NEG, F32), jnp.zeros((tq, 1), F32), jnp.zeros((tq, HD_C), F32))
                 for _ in range(nh))
    carry = lax.fori_loop(0, i, lambda j, c: chunk(j, c, False), init)
    carry = chunk(i, carry, True)
    for u in range(nh):
        _, l, acc = carry[u]
        o_ref[:, lanes[u]] = (acc / l * _silu(g_ref[:, lanes[u]])).astype(o_ref.dtype)


def fox_prompt(q, k, v, gate, ct, n_batch, t):
    tq = FOX_TILE
    nq = t // tq
    wide = FOX_HEADS_PER_STEP * HD_C
    ct4 = ct.reshape(n_batch, H_C, nq, tq)
    return pl.pallas_call(
        _fox_prompt_kernel,
        out_shape=jax.ShapeDtypeStruct((n_batch * t, W_C), BF16),
        grid=(n_batch, H_C // FOX_HEADS_PER_STEP, nq),
        in_specs=[pl.BlockSpec((tq, wide), lambda b, h, i: (b * nq + i, h)),
                  pl.BlockSpec((t, wide), lambda b, h, i: (b, h)),
                  pl.BlockSpec((t, wide), lambda b, h, i: (b, h)),
                  pl.BlockSpec((tq, wide), lambda b, h, i: (b * nq + i, h)),
                  pl.BlockSpec((None, H_C, nq, tq), lambda b, h, i: (b, 0, 0, 0))],
        out_specs=pl.BlockSpec((tq, wide), lambda b, h, i: (b * nq + i, h)),
        scratch_shapes=[pltpu.VMEM((t, wide), BF16), pltpu.VMEM((t, wide), BF16)],
        compiler_params=_cparams(("parallel", "parallel", "arbitrary")),
        name="fox_prompt",
    )(q, k, v, gate, ct4)


FOX_PAGES_PER_STEP = 8
FOX_POS_PER_VREG = LANES // H_C


def _split3(x):
    hi = x.astype(BF16)
    r1 = x - hi.astype(F32)
    mid = r1.astype(BF16)
    lo = (r1 - mid.astype(F32)).astype(BF16)
    return hi, mid, lo


def _head_allreduce(x, op):
    shift = H_C
    while shift < LANES:
        x = op(x, pltpu.roll(x, shift, axis=1))
        shift *= 2
    return x


def _fox_decode_kernel(pt_ref, q_ref, kn_ref, vn_ref, g_ref, cn_ref, *rest, n_new):
    pp = FOX_PAGES_PER_STEP
    k_refs, v_refs, lf_refs = rest[:pp], rest[pp:2 * pp], rest[2 * pp:3 * pp]
    o_ref = rest[3 * pp]
    q_sc, m_sc, l_sc, acc_sc, carry_sc = rest[3 * pp + 1:]
    step = pl.program_id(1)
    lane = lax.broadcasted_iota(jnp.int32, (n_new, LANES), 1)
    head_of_lane = lane % H_C
    is_head = [head_of_lane == h for h in range(H_C)]

    def tree(op, xs):
        while len(xs) > 1:
            xs = [op(xs[i], xs[i + 1]) if i + 1 < len(xs) else xs[i] for i in range(0, len(xs), 2)]
        return xs[0]

    def attend(blocks, mask):
        s_blocks = []
        for k_ref, _, bias in blocks:
            n_col = k_ref.shape[0] // LANES
            r = _dot_nt(q_sc[...], k_ref[...].astype(BF16))
            s_cols = []
            for c in range(n_col):
                e = r[0:n_new, c * LANES:(c + 1) * LANES]
                for h in range(1, H_C):
                    e = jnp.where(is_head[h], r[h * n_new:(h + 1) * n_new, c * LANES:(c + 1) * LANES], e)
                e = e + bias[c:c + 1, :]
                if mask is not None:
                    e = jnp.where(mask, e, NEG)
                s_cols.append(e)
            s_blocks.append(s_cols)
        m_prev = m_sc[...]
        m_all = tree(jnp.maximum, [e for s_cols in s_blocks for e in s_cols])
        m_new = jnp.maximum(m_prev, _head_allreduce(m_all, jnp.maximum))
        alpha = jnp.exp(m_prev - m_new)
        p_blocks = [[jnp.exp(e - m_new) for e in s_cols] for s_cols in s_blocks]
        p_sum = tree(jnp.add, [p for p_cols in p_blocks for p in p_cols])
        m_sc[...] = m_new
        l_sc[...] = l_sc[...] * alpha + _head_allreduce(p_sum, jnp.add)
        pvs = []
        for (_, v_ref, _), p_cols in zip(blocks, p_blocks):
            p_rows = [jnp.concatenate([jnp.where(is_head[h], p, 0.0) for p in p_cols], axis=1)
                      for h in range(H_C)]
            pvs.append(_dot(jnp.concatenate(p_rows, axis=0).astype(BF16), v_ref[...].astype(BF16)))
        pv = tree(jnp.add, pvs)
        for h in range(H_C):
            a_h = jnp.broadcast_to(alpha[:, h:h + 1], (n_new, HD_C))
            acc_sc[h] = acc_sc[h] * a_h + pv[h * n_new:(h + 1) * n_new, :]

    @pl.when(step == 0)
    def _():
        q_sc[...] = (q_ref[...] * (HD_C ** -0.5)).astype(BF16)
        m_sc[...] = jnp.full_like(m_sc, NEG)
        l_sc[...] = jnp.zeros_like(l_sc)
        acc_sc[...] = jnp.zeros_like(acc_sc)
        carry_sc[...] = jnp.zeros_like(carry_sc)
        q_i = lax.broadcasted_iota(jnp.int32, (n_new, LANES), 0)
        attend([(kn_ref, vn_ref, -cn_ref[...])], lane // H_C <= q_i)

    n = pp * (PAGE // FOX_POS_PER_VREG)
    x = jnp.concatenate([lf_refs[u][...] for u in range(pp)], axis=0)
    a_i = lax.broadcasted_iota(jnp.int32, (LANES, 2 * LANES), 0)
    b_i = lax.broadcasted_iota(jnp.int32, (LANES, 2 * LANES), 1)
    same_head = a_i % H_C == b_i % H_C
    hw = (same_head & ((b_i < LANES) | (a_i // H_C > (b_i - LANES) // H_C))).astype(BF16)
    z = _dot(jnp.concatenate(_split3(x), axis=0), hw)
    z = z[0:n] + z[n:2 * n] + z[2 * n:3 * n]
    y, within = z[:, :LANES], z[:, LANES:]
    r_i = lax.broadcasted_iota(jnp.int32, (n, n), 0)
    c_i = lax.broadcasted_iota(jnp.int32, (n, n), 1)
    groups = PAGE // FOX_POS_PER_VREG
    later = ((c_i // groups < r_i // groups) | ((c_i // groups == r_i // groups) & (c_i > r_i))).astype(BF16)
    t = _dot(later, jnp.concatenate(_split3(y), axis=1))
    bias_all = t[:, :LANES] + t[:, LANES:2 * LANES] + t[:, 2 * LANES:] + within + carry_sc[...]
    carry_sc[...] = carry_sc[...] + jnp.sum(y, axis=0, keepdims=True)
    attend([(k_refs[u], v_refs[u], bias_all[u * groups:(u + 1) * groups, :]) for u in range(pp)], None)

    @pl.when(step == pl.num_programs(1) - 1)
    def _():
        inv_l = 1.0 / l_sc[...]
        o = jnp.concatenate([acc_sc[h] * jnp.broadcast_to(inv_l[:, h:h + 1], (n_new, HD_C))
                             for h in range(H_C)], axis=1)
        o_ref[...] = o * _silu(g_ref[...])


def fox_decode(q_rows, k_new, v_new, gate, cnew, cache_k, cache_v, lf_pool, page_table):
    n_batch, n_new, _ = gate.shape
    assert n_new * H_C == LANES
    n_pages = page_table.shape[1]
    pp = FOX_PAGES_PER_STEP
    n_steps = n_pages // pp
    groups = PAGE // FOX_POS_PER_VREG

    def page(u):
        return lambda b, i, pt: (pt[b * n_pages + n_pages - 1 - (i * pp + u)], 0, 0)

    per_batch = lambda b, i, pt: (b, 0, 0)
    in_specs = [pl.BlockSpec((None, LANES, HD_C), per_batch),
                pl.BlockSpec((None, LANES, HD_C), per_batch),
                pl.BlockSpec((None, LANES, HD_C), per_batch),
                pl.BlockSpec((None, n_new, W_C), per_batch),
                pl.BlockSpec((None, 1, LANES), per_batch)]
    in_specs += [pl.BlockSpec((None, PAGE * H_C, HD_C), page(u)) for u in range(pp)]
    in_specs += [pl.BlockSpec((None, PAGE * H_C, HD_C), page(u)) for u in range(pp)]
    in_specs += [pl.BlockSpec((None, groups, LANES), page(u)) for u in range(pp)]
    grid_spec = pltpu.PrefetchScalarGridSpec(
        num_scalar_prefetch=1,
        grid=(n_batch, n_steps),
        in_specs=in_specs,
        out_specs=pl.BlockSpec((None, n_new, W_C), per_batch),
        scratch_shapes=[pltpu.VMEM((LANES, HD_C), BF16),
                        pltpu.VMEM((n_new, LANES), F32), pltpu.VMEM((n_new, LANES), F32),
                        pltpu.VMEM((H_C, n_new, HD_C), F32), pltpu.VMEM((1, LANES), F32)])
    return pl.pallas_call(
        functools.partial(_fox_decode_kernel, n_new=n_new),
        out_shape=jax.ShapeDtypeStruct((n_batch, n_new, W_C), F32),
        grid_spec=grid_spec,
        compiler_params=_cparams(("parallel", "arbitrary")),
        name="fox_decode",
    )(page_table.reshape(-1), q_rows, k_new, v_new, gate, cnew,
      *([cache_k] * pp), *([cache_v] * pp), *([lf_pool] * pp))


RWKV_GROUP = 256
RWKV_BATCH_PER_STEP = 2
RWKV_STATE_BLOCK = (H_A // 2, LANES, LANES)


def _seg_sum(x, gmat):
    n = x.shape[0]
    n_grp = W_A // RWKV_GROUP
    parts = [x[:, c * RWKV_GROUP:(c + 1) * RWKV_GROUP] for c in range(n_grp)]
    pieces = _split3(jnp.concatenate(parts, axis=0))
    r = _dot(jnp.concatenate(pieces, axis=0), gmat)
    m = n_grp * n
    r = r[0:m] + r[m:2 * m] + r[2 * m:3 * m]
    return jnp.concatenate([r[c * n:(c + 1) * n] for c in range(n_grp)], axis=1)


def _cumsum_rows(tri, x):
    w = x.shape[1]
    r = _dot(tri, jnp.concatenate(_split3(x), axis=1))
    return r[:, 0:w] + r[:, w:2 * w] + r[:, 2 * w:3 * w]


def _rwkv_kernel(z_ref, sp_ref, s0_ref, *refs):
    o_ref, so_ref, state_sc, prev_sc = refs[-4:]

    @pl.when(pl.program_id(1) == 0)
    def _():
        state_sc[...] = s0_ref[...]
        prev_sc[...] = sp_ref[...]

    res = [_rwkv_one(z_ref.at[u], *refs[:-4], state_sc.at[u], prev_sc.at[u]) for u in range(z_ref.shape[0])]
    o_ref[...] = jnp.stack([r[0] for r in res]).astype(o_ref.dtype)
    states = jnp.stack([r[1] for r in res])
    state_sc[...] = states
    so_ref[...] = states
    prev_sc[...] = jnp.stack([r[2] for r in res])


def _rwkv_one(z_ref, mu_ref, w0_ref, a0_ref, lora_ref, kk_ref, ka_ref, rk_ref,
              lng_ref, lnb_ref, state_sc, prev_sc):
    L = CHUNK
    n_in = z_ref.shape[0]
    z = z_ref[...]
    if n_in < L:
        z = jnp.concatenate([z, jnp.zeros((L - n_in, z.shape[1]), F32)], axis=0)
    cols = z[:, :SHIFT_W]
    gate = z[:, SHIFT_W:SHIFT_W + W_A]
    row = lax.broadcasted_iota(jnp.int32, (L, 1), 0)
    prev = jnp.where(row == 0, prev_sc[...], pltpu.roll(cols, 1, axis=0))
    xs = cols + (prev - cols) * mu_ref[...]
    r = xs[:, 0:W_A]
    k = xs[:, W_A:2 * W_A]
    v = xs[:, 2 * W_A:3 * W_A]
    lo = xs[:, 3 * W_A:SHIFT_W]
    lane = lax.broadcasted_iota(jnp.int32, (L, LANES), 1)
    first = lane < N_A
    lo = jnp.where(lane < LORA, jnp.tanh(lo), lo)
    lo_hi = lo.astype(BF16)
    lo_lo = (lo - lo_hi.astype(F32)).astype(BF16)
    up = _dot(jnp.concatenate([lo_hi, lo_hi, lo_lo], axis=1), lora_ref[...])
    logw = -float(np.exp(-0.5)) * jax.nn.sigmoid(w0_ref[...] + up[:, :W_A])
    lr = jax.nn.sigmoid(a0_ref[...] + up[:, W_A:])

    g_i = lax.broadcasted_iota(jnp.int32, (RWKV_GROUP, RWKV_GROUP), 0)
    g_j = lax.broadcasted_iota(jnp.int32, (RWKV_GROUP, RWKV_GROUP), 1)
    gmat = (g_i // N_A == g_j // N_A).astype(BF16)

    kk = k * kk_ref[...]
    kk = kk / jnp.maximum(jnp.sqrt(_seg_sum(kk * kk, gmat)), 1e-12)
    k = k * (1.0 + (lr - 1.0) * ka_ref[...])
    if n_in < L:
        real = row < n_in
        logw = jnp.where(real, logw, 0.0)
        kk = jnp.where(real, kk, 0.0)
        k = jnp.where(real, k, 0.0)

    t_i = lax.broadcasted_iota(jnp.int32, (L, L), 0)
    s_i = lax.broadcasted_iota(jnp.int32, (L, L), 1)
    cum = _cumsum_rows((s_i <= t_i).astype(BF16), logw)
    cum_l = cum[L - 1:L, :]
    g_incl = jnp.exp(cum)
    g_inv = jnp.exp(-cum)
    g_tail = jnp.exp(cum_l - cum)
    abar = -kk * jnp.exp(cum - logw)
    b = kk * lr
    bbar = b * g_inv
    kbar = k * g_inv
    rbar = r * g_incl
    btail = b * g_tail
    ktail = k * g_tail
    g_l = jnp.exp(cum_l)

    n2 = 2 * L
    r_i = lax.broadcasted_iota(jnp.int32, (n2, n2), 0)
    c_i = lax.broadcasted_iota(jnp.int32, (n2, n2), 1)
    same = r_i // L == c_i // L
    strict = same & (c_i % L < r_i % L)
    incl = same & (c_i % L <= r_i % L)
    eye = (r_i == c_i).astype(F32)

    def stack(x, sl):
        xp = x[:, sl]
        return jnp.concatenate([jnp.where(first, xp, 0.0), jnp.where(first, 0.0, xp)], axis=0)

    sls = [slice(p * LANES, (p + 1) * LANES) for p in range(H_A // 2)]
    bf = lambda xs_: [x.astype(BF16) for x in xs_]
    xar = bf([jnp.concatenate([stack(abar, sl), stack(rbar, sl)], axis=0) for sl in sls])
    xbk = bf([jnp.concatenate([stack(bbar, sl), stack(kbar, sl)], axis=0) for sl in sls])
    vst = bf([stack(v, sl) for sl in sls])
    tails = [jnp.concatenate([stack(btail, sl), stack(ktail, sl)], axis=0).astype(BF16) for sl in sls]
    big = [_dot_nt(l_, r_) for l_, r_ in zip(xar, xbk)]
    a_ab = [jnp.where(strict, g[:n2, :n2], 0.0) for g in big]
    a_rb = bf([jnp.where(incl, g[n2:, :n2], 0.0) for g in big])
    a_kk = bf([jnp.concatenate([jnp.where(strict, g[:n2, n2:], 0.0), jnp.where(incl, g[n2:, n2:], 0.0)], axis=0)
               for g in big])
    tinv = [eye + a for a in a_ab]
    pw = [_dot(x, x) for x in bf(a_ab)]
    n_steps = int(np.log2(L)) - 1
    for step in range(n_steps):
        pw_bf = bf(pw)
        if step == n_steps - 1:
            tinv = [t + _dot(t.astype(BF16), x) for t, x in zip(tinv, pw_bf)]
        else:
            both = [_dot(jnp.concatenate([x, t.astype(BF16)], axis=0), x) for x, t in zip(pw_bf, tinv)]
            pw = [m_[:n2] for m_ in both]
            tinv = [t + m_[n2:] for t, m_ in zip(tinv, both)]
    s_old = [state_sc[p] for p in range(H_A // 2)]
    ars = [_dot_nt(l_, s_) for l_, s_ in zip(xar, bf(s_old))]
    akv = [_dot(a_, v_) for a_, v_ in zip(a_kk, vst)]
    u_bf = bf([_dot(t.astype(BF16), (x[:n2] + w_[:n2]).astype(BF16)) for t, x, w_ in zip(tinv, ars, akv)])
    yst = [x[n2:] + w_[n2:] + _dot(rb_, u_) for x, w_, rb_, u_ in zip(ars, akv, a_rb, u_bf)]
    y = jnp.concatenate([t[:L] + t[L:] for t in yst], axis=1)
    upd = [_dot_tn(jnp.concatenate([u_, v_], axis=0), tl) for u_, v_, tl in zip(u_bf, vst, tails)]
    new_states = [s * g_l[:, sl] + d for s, sl, d in zip(s_old, sls, upd)]

    mean = _seg_sum(y, gmat) * (1.0 / N_A)
    dev = y - mean
    var = _seg_sum(dev * dev, gmat) * (1.0 / N_A)
    yn = dev * lax.rsqrt(var + GN_EPS) * lng_ref[...] + lnb_ref[...]
    bonus = _seg_sum(r * k * rk_ref[...], gmat) * v
    out = (yn + bonus) * _silu(gate)
    return out[:n_in], jnp.stack(new_states), cols[n_in - 1:n_in, :]


def rwkv7(z, shift_prev, s0_blk, mu, w0, w_up, a0, a_up, k_k, k_a, r_k, ln_g, ln_b, n_batch, t, out_dtype):
    n_in = min(t, CHUNK)
    nc = t // n_in
    bpg = RWKV_BATCH_PER_STEP
    assert n_batch % bpg == 0
    lora = jnp.zeros((2 * LORA, 2 * W_A), F32).at[:LORA, :W_A].set(w_up).at[LORA:, W_A:].set(a_up)
    lora_hi = lora.astype(BF16)
    lora_lo = (lora - lora_hi.astype(F32)).astype(BF16)
    lora3 = jnp.concatenate([lora_hi, lora_lo, lora_hi], axis=0)
    vec = lambda a, n: a.reshape(1, n)
    const = lambda shape: pl.BlockSpec(shape, lambda b, c: (0,) * len(shape))
    state_spec = pl.BlockSpec((bpg,) + RWKV_STATE_BLOCK, lambda b, c: (b, 0, 0, 0))
    out, s_blk = pl.pallas_call(
        _rwkv_kernel,
        out_shape=(jax.ShapeDtypeStruct((n_batch, t, W_A), out_dtype),
                   jax.ShapeDtypeStruct(s0_blk.shape, F32)),
        grid=(n_batch // bpg, nc),
        in_specs=[pl.BlockSpec((bpg, n_in, SHIFT_W + W_A), lambda b, c: (b, c, 0)),
                  pl.BlockSpec((bpg, 1, SHIFT_W), lambda b, c: (b, 0, 0)),
                  state_spec,
                  const((1, SHIFT_W)), const((1, W_A)), const((1, W_A)), const((6 * LORA, 2 * W_A)),
                  const((1, W_A)), const((1, W_A)), const((1, W_A)), const((1, W_A)), const((1, W_A))],
        out_specs=(pl.BlockSpec((bpg, n_in, W_A), lambda b, c: (b, c, 0)), state_spec),
        scratch_shapes=[pltpu.VMEM((bpg,) + RWKV_STATE_BLOCK, F32), pltpu.VMEM((bpg, 1, SHIFT_W), F32)],
        compiler_params=_cparams(("parallel", "arbitrary")),
        name="rwkv7",
    )(z.reshape(n_batch, t, SHIFT_W + W_A), shift_prev.reshape(n_batch, 1, SHIFT_W), s0_blk,
      vec(mu, SHIFT_W), vec(w0, W_A), vec(a0, W_A), lora3,
      vec(k_k, W_A), vec(k_a, W_A), vec(r_k, W_A), vec(ln_g, W_A), vec(ln_b, W_A))
    return out.reshape(n_batch * t, W_A), s_blk


def _state_blockdiag(s):
    b = s.shape[0]
    s = s.reshape(b, H_A // 2, 2, N_A, N_A)
    z = jnp.zeros_like(s[:, :, 0])
    top = jnp.concatenate([s[:, :, 0], z], axis=-1)
    bot = jnp.concatenate([z, s[:, :, 1]], axis=-1)
    return jnp.concatenate([top, bot], axis=-2)


def _state_diag_blocks(sb):
    b = sb.shape[0]
    h0 = sb[:, :, :N_A, :N_A]
    h1 = sb[:, :, N_A:, N_A:]
    return jnp.stack([h0, h1], axis=2).reshape(b, H_A, N_A, N_A)


_E_SIZES = (SHIFT_W, W_A, W_B, KV_B * HD_B, KV_B * HD_B, W_B, W_M, W_M)
_E_OFF = np.concatenate([[0], np.cumsum(_E_SIZES)])
_O_SIZES = (W_C, W_C, W_C, H_C, W_C, W_M, W_M)
_O_OFF = np.concatenate([[0], np.cumsum(_O_SIZES)])
_SWA_K_BLOCK = 2 * W_B // LANES
_SWA_V_BLOCK = _SWA_K_BLOCK + 1
_FM_F_BLOCK = 2 * W_M // LANES


def _ecol(w, i, j=None):
    return w[:, int(_E_OFF[i]):int(_E_OFF[(i if j is None else j) + 1])]


def _ocol(w, i, j=None):
    return w[:, int(_O_OFF[i]):int(_O_OFF[(i if j is None else j) + 1])]


def _even_layer(x2d, n_batch, t, mk, mv, shift_prev, s0_blk, swa_cache, w, out_dtype):
    xn = rmsnorm_cast(x2d, w["norm_e"])
    z_r = matmul(xn, w["w_rwkv"])
    z_s = matmul(xn, w["w_swa"])
    z_m = matmul(xn, w["w_mem0"])
    a_out, s_blk = rwkv7(z_r, shift_prev, s0_blk, *w["rwkv"], n_batch, t, out_dtype)
    new_shift = z_r.reshape(n_batch, t, -1)[:, -1, :SHIFT_W]
    k_new = z_s[:, 2 * W_B:2 * W_B + LANES].reshape(n_batch, t, LANES)
    v_new = z_s[:, 2 * W_B + LANES:].reshape(n_batch, t, LANES)
    if swa_cache is None:
        nq = t // WINDOW
        cur = lambda c: (lambda b, i: (b * nq + i, c))
        prev = lambda c: (lambda b, i: (b * nq + jnp.maximum(i - 1, 0), c))
        b_out = swa_attn(w["sink"], z_s, 0, 1, (z_s, prev(_SWA_K_BLOCK)), (z_s, cur(_SWA_K_BLOCK)),
                         (z_s, prev(_SWA_V_BLOCK)), (z_s, cur(_SWA_V_BLOCK)), n_batch, t, True, out_dtype)
        k_win, v_win = k_new[:, -WINDOW:], v_new[:, -WINDOW:]
    else:
        k_buf, v_buf = (c.reshape(n_batch, WINDOW, LANES) for c in swa_cache)
        pad = lambda a: jnp.pad(a, ((0, 0), (0, WINDOW - t), (0, 0)))
        per_b = lambda b, i: (b, 0, 0)
        b_out = swa_attn(w["sink"], z_s, 0, 1, (k_buf, per_b), (pad(k_new), per_b),
                         (v_buf, per_b), (pad(v_new), per_b), n_batch, t, False, out_dtype)
        k_win = jnp.concatenate([k_buf, k_new], axis=1)[:, -WINDOW:]
        v_win = jnp.concatenate([v_buf, v_new], axis=1)[:, -WINDOW:]
    m_out = mem_attn(z_m, 0, 1, mk, mv, n_batch, t, out_dtype)
    x1, xn1 = outproj([a_out, b_out, m_out], w["w_out_e"], x2d, w["norm_o"], False)
    kv_shape = (n_batch, WINDOW, KV_B, HD_B)
    return x1, xn1, (new_shift, _state_diag_blocks(s_blk), k_win.reshape(kv_shape), v_win.reshape(kv_shape))


def _odd_layer(x1, xn1, n_batch, t, mk, mv, fox_past, w, out_dtype):
    q = matmul(xn1, w["w_q"])
    k = matmul(xn1, w["w_k"])
    v = matmul(xn1, w["w_v"])
    gate = matmul(xn1, w["w_gc"])
    z_fm = matmul(xn1, w["w_fm"])
    if fox_past is None:
        logf, ct = fox_gates(z_fm, _FM_F_BLOCK, w["fox_bf"], n_batch, t, t)
        c_out = fox_prompt(q, k, v, gate, ct, n_batch, t)
    else:
        cache_k, cache_v, lf_pool, page_table = fox_past
        logf, cs = fox_gates(z_fm, _FM_F_BLOCK, w["fox_bf"], 1, n_batch * t, t)
        q_rows = jnp.swapaxes(q.reshape(n_batch, t, H_C, HD_C), 1, 2).reshape(n_batch, H_C * t, HD_C)
        pos_head_rows = lambda a: a.reshape(n_batch, t * H_C, HD_C)
        c_out = fox_decode(q_rows, pos_head_rows(k), pos_head_rows(v), gate.reshape(n_batch, t, W_C),
                           cs.reshape(n_batch, 1, t * H_C), cache_k, cache_v, lf_pool, page_table)
        c_out = c_out.reshape(n_batch * t, W_C)
    m_out = mem_attn(z_fm, 0, 1, mk, mv, n_batch, t, out_dtype)
    y = outproj([c_out, m_out], w["w_out_o"], x1, w["norm_f"], True)
    kv_shape = (n_batch, t, H_C, HD_C)
    return y, (k.reshape(kv_shape), v.reshape(kv_shape), logf.reshape(n_batch, t, H_C))


def kernel(x_prompt, x_sample, state_rwkv, state_shift, cache_swa_k, cache_swa_v, cache_mem_k, cache_mem_v, cache_fox_k, cache_fox_v, cache_fox_logf, page_table, mem_prompt, norm_e, w_in_e, rwkv_mu, rwkv_w0, rwkv_w_up, rwkv_a0, rwkv_a_up, rwkv_kk, rwkv_ka, rwkv_rk, rwkv_ln_g, rwkv_ln_b, swa_sink, w_out_e, norm_o, w_in_o, fox_bf, w_out_o, mem_norm, w_mem_k, w_mem_v, norm_f):
    n_b, t, d = x_prompt.shape
    n_db, s, _ = x_sample.shape
    assert state_rwkv.shape[0] == 1 and cache_fox_k.shape[0] == 1 and mem_norm.shape[0] == 2
    bf = lambda a: a.astype(BF16)
    we, wo = w_in_e[0], w_in_o[0]
    woe, woo = bf(w_out_e[0]), bf(w_out_o[0])
    w = {
        "norm_e": norm_e[0], "norm_o": norm_o[0], "norm_f": norm_f, "sink": swa_sink[0], "fox_bf": fox_bf[0],
        "w_rwkv": bf(_ecol(we, 0, 1)),
        "w_swa": bf(jnp.concatenate([_ecol(we, 2), _ecol(we, 5), _ecol(we, 3, 4)], axis=1)),
        "w_mem0": bf(_ecol(we, 6, 7)),
        "rwkv": (rwkv_mu[0], rwkv_w0[0], rwkv_w_up[0], rwkv_a0[0], rwkv_a_up[0], rwkv_kk[0], rwkv_ka[0],
                 rwkv_rk[0].reshape(-1), rwkv_ln_g[0], rwkv_ln_b[0]),
        "w_out_e": [woe[:W_A], woe[W_A:W_A + W_B], woe[W_A + W_B:]],
        "w_q": bf(_ocol(wo, 0)), "w_k": bf(_ocol(wo, 1)), "w_v": bf(_ocol(wo, 2)), "w_gc": bf(_ocol(wo, 4)),
        "w_fm": bf(jnp.concatenate([_ocol(wo, 5, 6), _ocol(wo, 3), jnp.zeros((d, LANES - H_C), F32)], axis=1)),
        "w_out_o": [woo[:W_C], woo[W_C:]],
    }

    p_mem = []
    for layer in range(2):
        hm = rmsnorm_cast(mem_prompt.reshape(n_b * N_MEM, d), mem_norm[layer])
        kv = matmul(hm, bf(jnp.concatenate([w_mem_k[layer], w_mem_v[layer]], axis=1)))
        p_mem.append((kv[:, :W_M].reshape(n_b, N_MEM, W_M), kv[:, W_M:].reshape(n_b, N_MEM, W_M)))
    xp = x_prompt.reshape(n_b * t, d)
    xp1, xpn1, (p_shift, p_rwkv, p_swa_k, p_swa_v) = _even_layer(
        xp, n_b, t, p_mem[0][0], p_mem[0][1], jnp.zeros((n_b, SHIFT_W), F32),
        jnp.zeros((n_b,) + RWKV_STATE_BLOCK, F32), None, w, BF16)
    y_prompt, (p_fox_k, p_fox_v, p_fox_lf) = _odd_layer(xp1, xpn1, n_b, t, p_mem[1][0], p_mem[1][1], None, w, BF16)

    xs = x_sample.reshape(n_db * s, d)
    memc = lambda c, layer: c[layer].reshape(n_db, N_MEM, W_M)
    xs1, xsn1, (s_shift, s_rwkv, s_swa_k, s_swa_v) = _even_layer(
        xs, n_db, s, memc(cache_mem_k, 0), memc(cache_mem_v, 0), state_shift[0], _state_blockdiag(state_rwkv[0]),
        (cache_swa_k[0], cache_swa_v[0]), w, F32)
    n_pool = cache_fox_k.shape[1]
    fox_past = (cache_fox_k.reshape(n_pool, PAGE * H_C, HD_C), cache_fox_v.reshape(n_pool, PAGE * H_C, HD_C),
                cache_fox_logf.reshape(n_pool, PAGE // FOX_POS_PER_VREG, LANES), page_table)
    y_sample, (s_fox_k, s_fox_v, s_fox_lf) = _odd_layer(xs1, xsn1, n_db, s, memc(cache_mem_k, 1), memc(cache_mem_v, 1),
                                                       fox_past, w, F32)

    mem_shape = (n_b, N_MEM, H_M, HD_M)
    return (y_prompt.reshape(n_b, t, d), y_sample.reshape(n_db, s, d),
            p_rwkv[None], p_shift[None], p_swa_k[None], p_swa_v[None],
            jnp.stack([p_mem[0][0].reshape(mem_shape), p_mem[1][0].reshape(mem_shape)]),
            jnp.stack([p_mem[0][1].reshape(mem_shape), p_mem[1][1].reshape(mem_shape)]),
            p_fox_k[None], p_fox_v[None], p_fox_lf[None],
            s_rwkv[None], s_shift[None], s_swa_k[None], s_swa_v[None],
            s_fox_k[None], s_fox_v[None], s_fox_lf[None])
```
